```python
import jax, jax.numpy as jnp
from jax import lax
import numpy as np

D_MODEL = 2048
BATCH = 4
SEQ = 2048
DEPTH = 1

D_MIX = D_MODEL
HG_HEADS = 8
HG_KEY = 128
HG_VAL = 128
HG_KW = HG_HEADS * HG_KEY
HG_WIDTH = HG_HEADS * HG_VAL
ATT_HEADS = 8
ATT_HEAD_DIM = 128
ATT_WIDTH = ATT_HEADS * ATT_HEAD_DIM
DILATED_PATTERNS = ((128, 1), (512, 4), (2048, 16))
N_BACK = 128
ATT_BLOCK = N_BACK
CHUNK = 64
D_FF = 5632
CONV_WIDTH = 3
EPS = 1e-6
IN_SPLITS = (HG_KW, 2 * HG_KW, 2 * HG_KW + HG_WIDTH, 2 * HG_KW + 2 * HG_WIDTH,
             2 * HG_KW + 2 * HG_WIDTH + ATT_WIDTH, 2 * HG_KW + 2 * HG_WIDTH + 2 * ATT_WIDTH)
IN_COLS = 2 * HG_KW + 2 * HG_WIDTH + 3 * ATT_WIDTH

kernel_name = "hymba_hgrn2_dilated_alibi_convffn_adaln"


def _rms(x, w):
    xf = x.astype(jnp.float32)
    y = xf * lax.rsqrt(jnp.mean(xf * xf, axis=-1, keepdims=True) + EPS)
    return (y * w.astype(jnp.float32)).astype(x.dtype)


def _hgrn2(hq, hf, hi, hg, lb, norm_w):
    b, s, _ = hq.shape
    f32 = jnp.float32
    q = jax.nn.silu(hq.astype(f32)).reshape(b, s, HG_HEADS, HG_KEY)
    f = lb.astype(f32) + (1.0 - lb.astype(f32)) * jax.nn.sigmoid(hf.astype(f32))
    f = f.reshape(b, s, HG_HEADS, HG_KEY)
    g = jnp.log(f)
    k = 1.0 - f
    v = hi.astype(f32).reshape(b, s, HG_HEADS, HG_VAL)
    nc = s // CHUNK

    def to_chunks(t):
        return t.reshape(b, nc, CHUNK, HG_HEADS, t.shape[-1]).transpose(1, 0, 3, 2, 4)

    causal = jnp.tril(jnp.ones((CHUNK, CHUNK), dtype=bool))

    def step(state, inp):
        qc, kc, vc, gc = inp
        G = jnp.cumsum(gc, axis=2)
        o_inter = jnp.einsum('bhtk,bhkv->bhtv', qc * jnp.exp(G), state)
        diff = G[:, :, :, None, :] - G[:, :, None, :, :]
        decay = jnp.exp(jnp.where(causal[:, :, None], diff, -jnp.inf))
        A = jnp.einsum('bhtk,bhtsk,bhsk->bhts', qc, decay, kc)
        o_intra = jnp.einsum('bhts,bhsv->bhtv', A, vc)
        G_last = G[:, :, -1:, :]
        state = (jnp.exp(G_last[:, :, 0, :])[..., None] * state
                 + jnp.einsum('bhsk,bhsv->bhkv', kc * jnp.exp(G_last - G), vc))
        return state, o_inter + o_intra

    state0 = jnp.zeros((b, HG_HEADS, HG_KEY, HG_VAL), f32)
    _, o = lax.scan(step, state0, (to_chunks(q), to_chunks(k), to_chunks(v), to_chunks(g)))
    o = o.transpose(1, 0, 3, 2, 4).reshape(b, s, HG_HEADS, HG_VAL)
    o = _rms(o, norm_w).reshape(b, s, HG_WIDTH) * jax.nn.silu(hg.astype(f32))
    return o.astype(hg.dtype)


def _dilated_branch(q, k, v, slopes, dil):
    b, s, h, dh = q.shape
    L = s // dil
    nblk = -(-L // ATT_BLOCK)
    Lp = nblk * ATT_BLOCK

    def split(t):
        t = t.reshape(b, L, dil, h, dh).transpose(0, 2, 3, 1, 4)
        t = jnp.pad(t, ((0, 0), (0, 0), (0, 0), (0, Lp - L), (0, 0)))
        return t.reshape(b, dil, h, nblk, ATT_BLOCK, dh)

    def with_prev(t):
        prev = jnp.pad(t, ((0, 0), (0, 0), (0, 0), (1, 0), (0, 0), (0, 0)))[:, :, :, :-1]
        return jnp.concatenate([prev, t], axis=4)

    qb = split(q)
    kc = with_prev(split(k))
    vc = with_prev(split(v))
    scores = jnp.einsum('brhnqd,brhnkd->brhnqk', qb, kc).astype(jnp.float32) * (dh ** -0.5)

    qi = jnp.arange(ATT_BLOCK)[:, None]
    kj = jnp.arange(2 * ATT_BLOCK)[None, :]
    steps = qi + ATT_BLOCK - kj
    blk = jnp.arange(nblk)[:, None, None]
    valid = (steps >= 0) & (steps <= N_BACK) & ((blk > 0) | (kj >= ATT_BLOCK))
    alibi = -slopes[:, None, None, None] * (steps * dil).astype(jnp.float32)[None, None]
    scores = jnp.where(valid, scores + alibi, -jnp.inf)
    m = jnp.max(scores, axis=-1, keepdims=True)
    p = jnp.exp(scores - m)
    den = jnp.sum(p, axis=-1, keepdims=True)
    o = jnp.einsum('brhnqk,brhnkd->brhnqd', p, vc.astype(jnp.float32)) / den
    lse = (m + jnp.log(den))[..., 0]

    def merge(t):
        t = t.reshape(b, dil, h, Lp, *t.shape[5:])[:, :, :, :L]
        t = jnp.moveaxis(t, 3, 1)
        return t.reshape(b, s, h, *t.shape[4:])

    return merge(o), merge(lse)


def _dilated_mixture(aq, ak, av, q_norm_w, k_norm_w):
    b, s, _ = aq.shape
    q = _rms(aq.reshape(b, s, ATT_HEADS, ATT_HEAD_DIM), q_norm_w)
    k = _rms(ak.reshape(b, s, ATT_HEADS, ATT_HEAD_DIM), k_norm_w)
    v = av.reshape(b, s, ATT_HEADS, ATT_HEAD_DIM)
    slopes = jnp.exp2(-8.0 * jnp.arange(1, ATT_HEADS + 1, dtype=jnp.float32) / ATT_HEADS)
    outs, lses = [], []
    for _, dil in DILATED_PATTERNS:
        o_p, l_p = _dilated_branch(q, k, v, slopes, dil)
        outs.append(o_p)
        lses.append(l_p)
    weights = jax.nn.softmax(jnp.stack(lses), axis=0)
    o = jnp.einsum('pbsh,pbshd->bshd', weights, jnp.stack(outs))
    return o.reshape(b, s, ATT_WIDTH).astype(aq.dtype)


def _causal_dwconv(a, w, bias):
    s = a.shape[1]
    ap = jnp.pad(a, ((0, 0), (CONV_WIDTH - 1, 0), (0, 0)))
    y = bias
    for j in range(CONV_WIDTH):
        y = y + ap[:, j:j + s] * w[j]
    return y


def _layer(x, mod, norm1_w, w_in, lb, hg_norm_w, q_norm_w, k_norm_w, w_out,
           norm2_w, w_up, conv_w, conv_b, w_down):
    shift1, scale1, gate1, shift2, scale2, gate2 = jnp.split(mod, 6, axis=-1)
    h = _rms(x, norm1_w) * (1.0 + scale1[:, None]) + shift1[:, None]
    proj = h @ w_in
    hq, hf, hi, hg, aq, ak, av = jnp.split(proj, IN_SPLITS, axis=-1)
    a_out = _hgrn2(hq, hf, hi, hg, lb, hg_norm_w)
    b_out = _dilated_mixture(aq, ak, av, q_norm_w, k_norm_w)
    mix = jnp.concatenate([a_out, b_out], axis=-1) @ w_out
    x = x + gate1[:, None] * mix

    h2 = _rms(x, norm2_w) * (1.0 + scale2[:, None]) + shift2[:, None]
    u = h2 @ w_up
    a, g = jnp.split(u, 2, axis=-1)
    y = jax.nn.silu(_causal_dwconv(a, conv_w, conv_b)) * g
    return x + gate2[:, None] * (y @ w_down)


def setup_inputs(seed: int = 0) -> dict:
    key = jax.random.key(seed)
    ks = jax.random.split(key, 16)
    f32 = jnp.float32
    nrm = lambda k, shp, sc: jax.random.normal(k, shp, f32) * sc
    D = D_MODEL
    return {
        "x": nrm(ks[0], (BATCH, SEQ, D), 1.0),
        "c": nrm(ks[1], (BATCH, D), 1.0),
        "w_ada": nrm(ks[2], (DEPTH, D, 6 * D), D ** -0.5),
        "b_ada": nrm(ks[3], (DEPTH, 6 * D), 0.02),
        "norm1_w": 1.0 + nrm(ks[4], (DEPTH, D), 0.02),
        "w_in": nrm(ks[5], (DEPTH, D, IN_COLS), D ** -0.5),
        "lb_logits": nrm(ks[6], (DEPTH + 1, HG_KW), 1.0),
        "hg_norm_w": 1.0 + nrm(ks[7], (DEPTH, HG_VAL), 0.02),
        "q_norm_w": 1.0 + nrm(ks[8], (DEPTH, ATT_HEAD_DIM), 0.02),
        "k_norm_w": 1.0 + nrm(ks[9], (DEPTH, ATT_HEAD_DIM), 0.02),
        "w_out": nrm(ks[10], (DEPTH, D_MIX, D), D_MIX ** -0.5),
        "norm2_w": 1.0 + nrm(ks[11], (DEPTH, D), 0.02),
        "w_up": nrm(ks[12], (DEPTH, D, 2 * D_FF), D ** -0.5),
        "conv_w": nrm(ks[13], (DEPTH, CONV_WIDTH, D_FF), CONV_WIDTH ** -0.5),
        "conv_b": nrm(ks[14], (DEPTH, D_FF), 0.02),
        "w_down": nrm(ks[15], (DEPTH, D_FF, D), D_FF ** -0.5),
    }


def reference(x, c, w_ada, b_ada, norm1_w, w_in, lb_logits, hg_norm_w, q_norm_w, k_norm_w,
              w_out, norm2_w, w_up, conv_w, conv_b, w_down):
    lb_all = jnp.cumsum(jax.nn.softmax(lb_logits.astype(jnp.float32), axis=0), axis=0)
    c_act = jax.nn.silu(c)
    for l in range(DEPTH):
        mod = c_act @ w_ada[l] + b_ada[l]
        x = _layer(x, mod, norm1_w[l], w_in[l], lb_all[l], hg_norm_w[l], q_norm_w[l],
                   k_norm_w[l], w_out[l], norm2_w[l], w_up[l], conv_w[l], conv_b[l], w_down[l])
    return x
```

```python
import functools
import math

import numpy as np
import jax
import jax.numpy as jnp
from jax import lax
from jax.experimental import pallas as pl
from jax.experimental.pallas import tpu as pltpu

F32 = jnp.float32
BF16 = jnp.bfloat16
EPS = 1e-6

HEAD = 128
HG_HEADS = 8
ATT_HEADS = 8
N_BACK = 128
DILATIONS = (1, 4, 16)
CONV_WIDTH = 3
HG_CHUNK = 64
VMEM_LIMIT = 56 * 1024 * 1024


def _dot(a, b):
    return jnp.dot(a, b, preferred_element_type=F32)


def _dot_nt(a, b):
    return lax.dot_general(a, b, (((1,), (1,)), ((), ())), preferred_element_type=F32)


def _dot_tn(a, b):
    return lax.dot_general(a, b, (((0,), (0,)), ((), ())), preferred_element_type=F32)


def _silu(v):
    return v * jax.nn.sigmoid(v)


def _mod_kernel(c_ref, w_ref, b_ref, o_ref):
    ca = _silu(c_ref[...]).astype(BF16)
    o_ref[...] = _dot(ca, w_ref[...].astype(BF16)) + b_ref[...]


def _modulation(c_pad, w_ada, b_ada, tn=1024):
    rows, d = c_pad.shape
    n = w_ada.shape[1]
    return pl.pallas_call(
        _mod_kernel,
        grid=(n // tn,),
        in_specs=[pl.BlockSpec((rows, d), lambda j: (0, 0)),
                  pl.BlockSpec((d, tn), lambda j: (0, j)),
                  pl.BlockSpec((1, tn), lambda j: (0, j))],
        out_specs=pl.BlockSpec((rows, tn), lambda j: (0, j)),
        out_shape=jax.ShapeDtypeStruct((rows, n), F32),
        compiler_params=pltpu.CompilerParams(vmem_limit_bytes=VMEM_LIMIT),
        name="adaln_mod",
    )(c_pad, w_ada, b_ada)


def _norm_kernel(x_ref, w_ref, scale_ref, shift_ref, o_ref):
    x = x_ref[0]
    ms = jnp.mean(x * x, axis=-1, keepdims=True)
    y = x * lax.rsqrt(ms + EPS) * w_ref[...]
    o_ref[0] = (y * (1.0 + scale_ref[0]) + shift_ref[0]).astype(o_ref.dtype)


def _norm_modulate(x, w, scale, shift, ts=512):
    b, s, d = x.shape
    return pl.pallas_call(
        _norm_kernel,
        grid=(b, s // ts),
        in_specs=[pl.BlockSpec((1, ts, d), lambda i, j: (i, j, 0)),
                  pl.BlockSpec((1, d), lambda i, j: (0, 0)),
                  pl.BlockSpec((1, 1, d), lambda i, j: (i, 0, 0)),
                  pl.BlockSpec((1, 1, d), lambda i, j: (i, 0, 0))],
        out_specs=pl.BlockSpec((1, ts, d), lambda i, j: (i, j, 0)),
        out_shape=jax.ShapeDtypeStruct((b, s, d), BF16),
        compiler_params=pltpu.CompilerParams(vmem_limit_bytes=VMEM_LIMIT),
        name="norm_modulate",
    )(x, w, scale, shift)


def _inproj_kernel(x_ref, w_ref, o_ref, wb_ref):
    @pl.when(pl.program_id(1) == 0)
    def _():
        wb_ref[...] = w_ref[...].astype(BF16)

    o_ref[...] = _dot(x_ref[...], wb_ref[...])


def _in_projection(h, w, tm=1024, tn=1024):
    m, k = h.shape
    n = w.shape[1]
    return pl.pallas_call(
        _inproj_kernel,
        grid=(n // tn, m // tm),
        in_specs=[pl.BlockSpec((tm, k), lambda j, i: (i, 0)),
                  pl.BlockSpec((k, tn), lambda j, i: (0, j))],
        out_specs=pl.BlockSpec((tm, tn), lambda j, i: (i, j)),
        out_shape=jax.ShapeDtypeStruct((m, n), F32),
        scratch_shapes=[pltpu.VMEM((k, tn), BF16)],
        compiler_params=pltpu.CompilerParams(
            dimension_semantics=("arbitrary", "arbitrary"), vmem_limit_bytes=VMEM_LIMIT),
        name="in_proj",
    )(h, w)


def _hgrn_tables(c):
    t = np.arange(c)[:, None]
    u = np.arange(c)[None, :]
    tables = [(u <= t), (u > t)]
    masks = []
    m = c
    while m >= 2:
        half = m // 2
        start = (t // m) * m
        mid = start + half - 1
        upper = (t - start) >= half
        tables.append(np.where(upper, (u > mid) & (u <= t), (u > t) & (u <= mid)))
        u_start = (u // m) * m
        masks.append(upper & ((u - u_start) < half) & (u_start == start))
        m //= 2
    tab = np.concatenate(tables, axis=0).astype(np.float32)
    tab3 = np.concatenate([tab, tab, tab], axis=1)
    msk = np.stack(masks).astype(np.float32)
    return tab3, msk


def _hgrn_kernel(hq_ref, hf_ref, hi_ref, hg_ref, lbl_ref, nw_ref, tab_ref, msk_ref, o_ref, *, chunk, seq, layer):
    c = chunk
    levels = msk_ref.shape[0]
    lbl = lbl_ref[...]
    ex = jnp.exp(lbl - jnp.max(lbl, axis=0, keepdims=True))
    lb = jnp.sum(ex[0:layer + 1], axis=0, keepdims=True) / jnp.sum(ex, axis=0, keepdims=True)
    nw = nw_ref[...]
    tab = tab_ref[...]
    eye = (lax.broadcasted_iota(jnp.int32, (c, c), 0) == lax.broadcasted_iota(jnp.int32, (c, c), 1))

    def body(i, state_t):
        rows = pl.ds(pl.multiple_of(i * c, c), c)
        q = _silu(hq_ref[0, rows, :])
        f = lb + (1.0 - lb) * jax.nn.sigmoid(hf_ref[0, rows, :])
        g = jnp.log(f)
        kk = 1.0 - f
        v = hi_ref[0, rows, :].astype(BF16)

        g1 = g.astype(BF16)
        r1 = g - g1.astype(F32)
        g2 = r1.astype(BF16)
        g3 = (r1 - g2.astype(F32)).astype(BF16)
        sums = _dot(tab, jnp.concatenate([g1, g2, g3], axis=0))
        dec = jnp.exp(sums)

        e_cum = dec[0:c]
        o = _dot_nt((q * e_cum).astype(BF16), state_t.astype(BF16))

        a = jnp.where(eye, _dot_nt(q.astype(BF16), kk.astype(BF16)), 0.0)
        for l in range(levels):
            e = dec[(2 + l) * c:(3 + l) * c]
            sc = _dot_nt((q * e).astype(BF16), (kk * e).astype(BF16))
            a = jnp.where(msk_ref[l] > 0.5, sc, a)
        o = o + _dot(a.astype(BF16), v)

        k_dec = (kk * dec[c:2 * c]).astype(BF16)
        state_t = state_t * e_cum[c - 1:c, :] + _dot_tn(v, k_dec)

        ms = jnp.mean(o * o, axis=-1, keepdims=True)
        on = o * lax.rsqrt(ms + EPS) * nw
        o_ref[0, rows, :] = (on * _silu(hg_ref[0, rows, :])).astype(o_ref.dtype)
        return state_t

    lax.fori_loop(0, seq // c, body, jnp.zeros((HEAD, HEAD), F32))


def _hgrn(proj, lb_logits, norm_w, heads, col0, layer):
    b, s, _ = proj.shape
    nl = lb_logits.shape[0]
    tab3, msk = _hgrn_tables(HG_CHUNK)
    tab3 = jnp.asarray(tab3, BF16)
    msk = jnp.asarray(msk, F32)

    def col(group):
        return pl.BlockSpec((1, s, HEAD), lambda i, j, group=group: (i, 0, col0 + group * heads + j))

    return pl.pallas_call(
        functools.partial(_hgrn_kernel, chunk=HG_CHUNK, seq=s, layer=layer),
        grid=(b, heads),
        in_specs=[col(0), col(1), col(2), col(3),
                  pl.BlockSpec((nl, HEAD), lambda i, j: (0, j)),
                  pl.BlockSpec((1, HEAD), lambda i, j: (0, 0)),
                  pl.BlockSpec(tab3.shape, lambda i, j: (0, 0)),
                  pl.BlockSpec(msk.shape, lambda i, j: (0, 0, 0))],
        out_specs=pl.BlockSpec((1, s, HEAD), lambda i, j: (i, 0, j)),
        out_shape=jax.ShapeDtypeStruct((b, s, heads * HEAD), BF16),
        compiler_params=pltpu.CompilerParams(vmem_limit_bytes=VMEM_LIMIT),
        name="hgrn2",
    )(proj, proj, proj, proj, lb_logits, norm_w, tab3, msk)


def _attn_kernel(aq_ref, ak_ref, av_ref, qw_ref, kw_ref, slope_ref, o_ref, q_s, k_s, o_s, l_s, *, seq):
    blk = N_BACK
    npat = len(DILATIONS)
    slope = slope_ref[0]
    qw = qw_ref[...] * (HEAD ** -0.5)
    kw = kw_ref[...]
    rows_n = 256

    def norm_body(i, carry):
        rows = pl.ds(pl.multiple_of(i * rows_n, rows_n), rows_n)
        aq = aq_ref[0, rows, :]
        q_s[rows, :] = aq * lax.rsqrt(jnp.mean(aq * aq, axis=-1, keepdims=True) + EPS) * qw
        ak = ak_ref[0, rows, :]
        k_s[rows, :] = ak * lax.rsqrt(jnp.mean(ak * ak, axis=-1, keepdims=True) + EPS) * kw
        return carry

    lax.fori_loop(0, seq // rows_n, norm_body, 0)

    qi = lax.broadcasted_iota(jnp.int32, (blk, 2 * blk), 0)
    kj = lax.broadcasted_iota(jnp.int32, (blk, 2 * blk), 1)
    steps = qi + blk - kj
    ok = (steps >= 0) & (steps <= N_BACK)
    stepsf = steps.astype(F32)

    def one_block(p, dil, base, bias, with_prev):
        qb = q_s[pl.ds(base, blk, stride=dil), :].astype(BF16)
        if with_prev:
            kb = k_s[pl.ds(base - blk * dil, 2 * blk, stride=dil), :].astype(BF16)
            vb = av_ref[0, pl.ds(base - blk * dil, 2 * blk, stride=dil), :].astype(BF16)
            sc = _dot_nt(qb, kb) + bias
        else:
            kb = k_s[pl.ds(base, blk, stride=dil), :].astype(BF16)
            vb = av_ref[0, pl.ds(base, blk, stride=dil), :].astype(BF16)
            sc = _dot_nt(qb, kb) + bias[:, blk:]
        m = jnp.max(sc, axis=-1, keepdims=True)
        pr = jnp.exp(sc - m)
        den = jnp.sum(pr, axis=-1, keepdims=True)
        o = _dot(pr.astype(BF16), vb) / den
        o_s[p, pl.ds(base, blk, stride=dil), :] = o
        l_s[p, pl.ds(base, blk, stride=dil), :] = jnp.broadcast_to(m + jnp.log(den), (blk, HEAD))

    for p, dil in enumerate(DILATIONS):
        nblk = seq // (dil * blk)
        bias = jnp.where(ok, -slope[:, 0:1] * (stepsf * float(dil)), -jnp.inf)

        def res_body(r, carry, p=p, dil=dil, nblk=nblk, bias=bias):
            one_block(p, dil, r, bias, False)
            if nblk > 1:
                def blk_body(n, c2):
                    one_block(p, dil, r + n * (blk * dil), bias, True)
                    return c2
                lax.fori_loop(1, nblk, blk_body, 0)
            return carry

        lax.fori_loop(0, dil, res_body, 0)

    def merge_body(i, carry):
        rows = pl.ds(pl.multiple_of(i * rows_n, rows_n), rows_n)
        ls = [l_s[p, rows, :] for p in range(npat)]
        m = functools.reduce(jnp.maximum, ls)
        es = [jnp.exp(l - m) for l in ls]
        den = functools.reduce(lambda a, b: a + b, es)
        num = functools.reduce(lambda a, b: a + b, [e * o_s[p, rows, :] for p, e in enumerate(es)])
        o_ref[0, rows, :] = (num / den).astype(o_ref.dtype)
        return carry

    lax.fori_loop(0, seq // rows_n, merge_body, 0)


def _attention(proj, q_norm_w, k_norm_w, heads, col0):
    b, s, _ = proj.shape
    slopes = jnp.exp2(-8.0 * jnp.arange(1, heads + 1, dtype=F32) / heads)
    slopes = jnp.broadcast_to(slopes[:, None, None], (heads, 1, HEAD))

    def col(group):
        return pl.BlockSpec((1, s, HEAD), lambda i, j, group=group: (i, 0, col0 + group * heads + j))

    npat = len(DILATIONS)
    return pl.pallas_call(
        functools.partial(_attn_kernel, seq=s),
        grid=(b, heads),
        in_specs=[col(0), col(1), col(2),
                  pl.BlockSpec((1, HEAD), lambda i, j: (0, 0)),
                  pl.BlockSpec((1, HEAD), lambda i, j: (0, 0)),
                  pl.BlockSpec((1, 1, HEAD), lambda i, j: (j, 0, 0))],
        out_specs=pl.BlockSpec((1, s, HEAD), lambda i, j: (i, 0, j)),
        out_shape=jax.ShapeDtypeStruct((b, s, heads * HEAD), BF16),
        scratch_shapes=[pltpu.VMEM((s, HEAD), F32), pltpu.VMEM((s, HEAD), F32),
                        pltpu.VMEM((npat, s, HEAD), F32), pltpu.VMEM((npat, s, HEAD), F32)],
        compiler_params=pltpu.CompilerParams(vmem_limit_bytes=VMEM_LIMIT),
        name="dilated_attn",
    )(proj, proj, proj, q_norm_w, k_norm_w, slopes)


def _outproj_kernel(a_ref, b_ref, wa_ref, wb_ref, x_ref, gate_ref, o_ref, wa_s, wb_s):
    @pl.when(pl.program_id(1) == 0)
    def _():
        wa_s[...] = wa_ref[...].astype(BF16)
        wb_s[...] = wb_ref[...].astype(BF16)

    mix = _dot(a_ref[...], wa_s[...]) + _dot(b_ref[...], wb_s[...])
    o_ref[...] = x_ref[...] + gate_ref[0] * mix


def _out_projection(a, b, w, x, gate, seq, tm=1024, tn=1024):
    m, ka = a.shape
    kb = b.shape[1]
    n = w.shape[1]
    per_seq = seq // tm
    return pl.pallas_call(
        _outproj_kernel,
        grid=(n // tn, m // tm),
        in_specs=[pl.BlockSpec((tm, ka), lambda j, i: (i, 0)),
                  pl.BlockSpec((tm, kb), lambda j, i: (i, 0)),
                  pl.BlockSpec((ka, tn), lambda j, i: (0, j)),
                  pl.BlockSpec((kb, tn), lambda j, i: (ka // kb, j)),
                  pl.BlockSpec((tm, tn), lambda j, i: (i, j)),
                  pl.BlockSpec((1, 1, tn), lambda j, i: (i // per_seq, 0, j))],
        out_specs=pl.BlockSpec((tm, tn), lambda j, i: (i, j)),
        out_shape=jax.ShapeDtypeStruct((m, n), F32),
        scratch_shapes=[pltpu.VMEM((ka, tn), BF16), pltpu.VMEM((kb, tn), BF16)],
        compiler_params=pltpu.CompilerParams(
            dimension_semantics=("arbitrary", "arbitrary"), vmem_limit_bytes=VMEM_LIMIT),
        name="out_proj",
    )(a, b, w, w, x, gate)


def _up_kernel(h_ref, wa_ref, wg_ref, cw_ref, cb_ref, y_ref, wa_s, wg_s, halo_s, *, per_seq):
    i = pl.program_id(1)

    @pl.when(i == 0)
    def _():
        wa_s[...] = wa_ref[...].astype(BF16)
        wg_s[...] = wg_ref[...].astype(BF16)

    @pl.when(i % per_seq == 0)
    def _():
        halo_s[...] = jnp.zeros_like(halo_s)

    h = h_ref[...]
    a = _dot(h, wa_s[...])
    g = _dot(h, wg_s[...])
    tm = a.shape[0]
    prev = halo_s[...]
    p1 = prev[7:8, :]
    p2 = prev[6:7, :]
    row = lax.broadcasted_iota(jnp.int32, a.shape, 0)
    a1 = jnp.where(row == 0, p1, pltpu.roll(a, 1, axis=0))
    a2 = jnp.where(row == 0, p2, jnp.where(row == 1, p1, pltpu.roll(a, 2, axis=0)))
    halo_s[...] = a[tm - 8:tm, :]
    cw = cw_ref[...]
    y = cb_ref[...] + a2 * cw[0:1, :] + a1 * cw[1:2, :] + a * cw[2:3, :]
    y_ref[...] = (_silu(y) * g).astype(y_ref.dtype)


def _up_projection(h, w, conv_w, conv_b, seq, tm=1024, tn=512):
    m, k = h.shape
    dff = w.shape[1] // 2
    nb = dff // tn
    return pl.pallas_call(
        functools.partial(_up_kernel, per_seq=seq // tm),
        grid=(nb, m // tm),
        in_specs=[pl.BlockSpec((tm, k), lambda j, i: (i, 0)),
                  pl.BlockSpec((k, tn), lambda j, i: (0, j)),
                  pl.BlockSpec((k, tn), lambda j, i: (0, nb + j)),
                  pl.BlockSpec((CONV_WIDTH, tn), lambda j, i: (0, j)),
                  pl.BlockSpec((1, tn), lambda j, i: (0, j))],
        out_specs=pl.BlockSpec((tm, tn), lambda j, i: (i, j)),
        out_shape=jax.ShapeDtypeStruct((m, dff), BF16),
        scratch_shapes=[pltpu.VMEM((k, tn), BF16), pltpu.VMEM((k, tn), BF16), pltpu.VMEM((8, tn), F32)],
        compiler_params=pltpu.CompilerParams(
            dimension_semantics=("arbitrary", "arbitrary"), vmem_limit_bytes=VMEM_LIMIT),
        name="up_proj_conv_gate",
    )(h, w, w, conv_w, conv_b)


def _down_kernel(y_ref, w_ref, x_ref, gate_ref, o_ref, w_s):
    @pl.when(pl.program_id(1) == 0)
    def _():
        w_s[...] = w_ref[...].astype(BF16)

    o_ref[...] = x_ref[...] + gate_ref[0] * _dot(y_ref[...], w_s[...])


def _down_projection(y, w, x, gate, seq, tm=512, tn=512):
    m, k = y.shape
    n = w.shape[1]
    per_seq = seq // tm
    return pl.pallas_call(
        _down_kernel,
        grid=(n // tn, m // tm),
        in_specs=[pl.BlockSpec((tm, k), lambda j, i: (i, 0)),
                  pl.BlockSpec((k, tn), lambda j, i: (0, j)),
                  pl.BlockSpec((tm, tn), lambda j, i: (i, j)),
                  pl.BlockSpec((1, 1, tn), lambda j, i: (i // per_seq, 0, j))],
        out_specs=pl.BlockSpec((tm, tn), lambda j, i: (i, j)),
        out_shape=jax.ShapeDtypeStruct((m, n), F32),
        scratch_shapes=[pltpu.VMEM((k, tn), BF16)],
        compiler_params=pltpu.CompilerParams(
            dimension_semantics=("arbitrary", "arbitrary"), vmem_limit_bytes=VMEM_LIMIT),
        name="down_proj",
    )(y, w, x, gate)


def _layer(layer, x, mod, norm1_w, w_in, lb_logits, hg_norm_w, q_norm_w, k_norm_w, w_out, norm2_w, w_up, conv_w,
           conv_b, w_down):
    b, s, d = x.shape
    shift1, scale1, gate1, shift2, scale2, gate2 = [mod[:b, j * d:(j + 1) * d].reshape(b, 1, d) for j in range(6)]

    h = _norm_modulate(x, norm1_w.reshape(1, d), scale1, shift1)
    proj = _in_projection(h.reshape(b * s, d), w_in).reshape(b, s, -1)
    a_out = _hgrn(proj, lb_logits, hg_norm_w.reshape(1, HEAD), HG_HEADS, 0, layer)
    b_out = _attention(proj, q_norm_w.reshape(1, HEAD), k_norm_w.reshape(1, HEAD), ATT_HEADS, 4 * HG_HEADS)
    x1 = _out_projection(a_out.reshape(b * s, -1), b_out.reshape(b * s, -1), w_out, x.reshape(b * s, d), gate1, s)

    h2 = _norm_modulate(x1.reshape(b, s, d), norm2_w.reshape(1, d), scale2, shift2)
    y = _up_projection(h2.reshape(b * s, d), w_up, conv_w, conv_b.reshape(1, -1), s)
    x2 = _down_projection(y, w_down, x1, gate2, s)
    return x2.reshape(b, s, d)


def kernel(x, c, w_ada, b_ada, norm1_w, w_in, lb_logits, hg_norm_w, q_norm_w, k_norm_w, w_out, norm2_w, w_up, conv_w, conv_b, w_down):
    depth = w_ada.shape[0]
    b = x.shape[0]
    c_pad = jnp.pad(c, ((0, 8 - b), (0, 0)))
    for l in range(depth):
        mod = _modulation(c_pad, w_ada[l], b_ada[l].reshape(1, -1))
        x = _layer(l, x, mod, norm1_w[l], w_in[l], lb_logits, hg_norm_w[l], q_norm_w[l], k_norm_w[l],
                   w_out[l], norm2_w[l], w_up[l], conv_w[l], conv_b[l], w_down[l])
    return x
```

```python
import functools

import numpy as np
import jax
import jax.numpy as jnp
from jax import lax
from jax.experimental import pallas as pl
from jax.experimental.pallas import tpu as pltpu

F32 = jnp.float32
BF16 = jnp.bfloat16
EPS = 1e-6

HEAD = 128
HG_HEADS = 8
ATT_HEADS = 8
N_BACK = 128
DILATIONS = (1, 4, 16)
CONV_WIDTH = 3
HG_CHUNK = 64
VMEM_LIMIT = 56 * 1024 * 1024


def _dot(a, b):
    return jnp.dot(a, b, preferred_element_type=F32)


def _dot_nt(a, b):
    return lax.dot_general(a, b, (((1,), (1,)), ((), ())), preferred_element_type=F32)


def _dot_tn(a, b):
    return lax.dot_general(a, b, (((0,), (0,)), ((), ())), preferred_element_type=F32)


def _silu(v):
    return v * jax.nn.sigmoid(v)


def _mod_kernel(c_ref, w_ref, b_ref, o_ref):
    ca = _silu(c_ref[...]).astype(BF16)
    o_ref[...] = _dot(ca, w_ref[...].astype(BF16)) + b_ref[...]


def _modulation(c_pad, w_ada, b_ada, tn=1024):
    rows, d = c_pad.shape
    n = w_ada.shape[1]
    return pl.pallas_call(
        _mod_kernel,
        grid=(n // tn,),
        in_specs=[pl.BlockSpec((rows, d), lambda j: (0, 0)),
                  pl.BlockSpec((d, tn), lambda j: (0, j)),
                  pl.BlockSpec((1, tn), lambda j: (0, j))],
        out_specs=pl.BlockSpec((rows, tn), lambda j: (0, j)),
        out_shape=jax.ShapeDtypeStruct((rows, n), F32),
        compiler_params=pltpu.CompilerParams(vmem_limit_bytes=VMEM_LIMIT),
        name="adaln_mod",
    )(c_pad, w_ada, b_ada)


def _norm_kernel(x_ref, w_ref, scale_ref, shift_ref, o_ref):
    x = x_ref[0]
    ms = jnp.mean(x * x, axis=-1, keepdims=True)
    y = x * lax.rsqrt(ms + EPS) * w_ref[...]
    o_ref[0] = (y * (1.0 + scale_ref[0]) + shift_ref[0]).astype(o_ref.dtype)


def _norm_modulate(x, w, scale, shift, ts=512):
    b, s, d = x.shape
    return pl.pallas_call(
        _norm_kernel,
        grid=(b, s // ts),
        in_specs=[pl.BlockSpec((1, ts, d), lambda i, j: (i, j, 0)),
                  pl.BlockSpec((1, d), lambda i, j: (0, 0)),
                  pl.BlockSpec((1, 1, d), lambda i, j: (i, 0, 0)),
                  pl.BlockSpec((1, 1, d), lambda i, j: (i, 0, 0))],
        out_specs=pl.BlockSpec((1, ts, d), lambda i, j: (i, j, 0)),
        out_shape=jax.ShapeDtypeStruct((b, s, d), BF16),
        compiler_params=pltpu.CompilerParams(vmem_limit_bytes=VMEM_LIMIT),
        name="norm_modulate",
    )(x, w, scale, shift)


def _inproj_kernel(x_ref, w_ref, o_ref, wb_ref):
    @pl.when(pl.program_id(1) == 0)
    def _():
        wb_ref[...] = w_ref[...].astype(BF16)

    o_ref[...] = _dot(x_ref[...], wb_ref[...])


def _in_projection(h, w, tm=1024, tn=1024):
    m, k = h.shape
    n = w.shape[1]
    return pl.pallas_call(
        _inproj_kernel,
        grid=(n // tn, m // tm),
        in_specs=[pl.BlockSpec((tm, k), lambda j, i: (i, 0)),
                  pl.BlockSpec((k, tn), lambda j, i: (0, j))],
        out_specs=pl.BlockSpec((tm, tn), lambda j, i: (i, j)),
        out_shape=jax.ShapeDtypeStruct((m, n), F32),
        scratch_shapes=[pltpu.VMEM((k, tn), BF16)],
        compiler_params=pltpu.CompilerParams(
            dimension_semantics=("arbitrary", "arbitrary"), vmem_limit_bytes=VMEM_LIMIT),
        name="in_proj",
    )(h, w)


def _hgrn_tables(c):
    t = np.arange(c)[:, None]
    u = np.arange(c)[None, :]
    tables = [(u <= t), (u > t)]
    masks = []
    m = c
    while m >= 2:
        half = m // 2
        start = (t // m) * m
        mid = start + half - 1
        upper = (t - start) >= half
        tables.append(np.where(upper, (u > mid) & (u <= t), (u > t) & (u <= mid)))
        u_start = (u // m) * m
        masks.append(upper & ((u - u_start) < half) & (u_start == start))
        m //= 2
    tab = np.concatenate(tables, axis=0).astype(np.float32)
    tab3 = np.concatenate([tab, tab, tab], axis=1)
    msk = np.stack(masks).astype(np.float32)
    return tab3, msk


def _hgrn_kernel(hq_ref, hf_ref, hi_ref, hg_ref, lbl_ref, nw_ref, tab_ref, msk_ref, o_ref, st_ref, *,
                 chunk, heads, layer):
    c = chunk
    levels = msk_ref.shape[0]
    ts = hq_ref.shape[1]

    @pl.when(pl.program_id(1) == 0)
    def _():
        st_ref[...] = jnp.zeros_like(st_ref)

    lbl = lbl_ref[...]
    ex = jnp.exp(lbl - jnp.max(lbl, axis=0, keepdims=True))
    lb_all = jnp.sum(ex[0:layer + 1], axis=0, keepdims=True) / jnp.sum(ex, axis=0, keepdims=True)
    nw = nw_ref[...]
    eye = (lax.broadcasted_iota(jnp.int32, (c, c), 0) == lax.broadcasted_iota(jnp.int32, (c, c), 1))

    def one_head(rows, h):
        cols = slice(h * HEAD, (h + 1) * HEAD)
        lb = lb_all[:, cols]
        q = _silu(hq_ref[0, rows, cols])
        f = lb + (1.0 - lb) * jax.nn.sigmoid(hf_ref[0, rows, cols])
        g = jnp.log(f)
        kk = 1.0 - f
        v = hi_ref[0, rows, cols].astype(BF16)

        g1 = g.astype(BF16)
        r1 = g - g1.astype(F32)
        g2 = r1.astype(BF16)
        g3 = (r1 - g2.astype(F32)).astype(BF16)
        sums = _dot(tab_ref[...], jnp.concatenate([g1, g2, g3], axis=0))
        dec = jnp.exp(sums)

        state_t = st_ref[h]
        e_cum = dec[0:c]
        o = _dot_nt((q * e_cum).astype(BF16), state_t.astype(BF16))

        a = jnp.where(eye, _dot_nt(q.astype(BF16), kk.astype(BF16)), 0.0)
        for l in range(levels):
            e = dec[(2 + l) * c:(3 + l) * c]
            sc = _dot_nt((q * e).astype(BF16), (kk * e).astype(BF16))
            a = jnp.where(msk_ref[l] > 0.5, sc, a)
        o = o + _dot(a.astype(BF16), v)

        k_dec = (kk * dec[c:2 * c]).astype(BF16)
        st_ref[h] = state_t * e_cum[c - 1:c, :] + _dot_tn(v, k_dec)

        ms = jnp.mean(o * o, axis=-1, keepdims=True)
        on = o * lax.rsqrt(ms + EPS) * nw
        o_ref[0, rows, cols] = (on * _silu(hg_ref[0, rows, cols])).astype(o_ref.dtype)

    def body(i, carry):
        rows = pl.ds(pl.multiple_of(i * c, c), c)
        for h in range(heads):
            one_head(rows, h)
        return carry

    lax.fori_loop(0, ts // c, body, 0)


def _hgrn(proj, lb_logits, norm_w, heads, layer, ts=512):
    b, s, _ = proj.shape
    nl = lb_logits.shape[0]
    w = heads * HEAD
    tab3, msk = _hgrn_tables(HG_CHUNK)
    tab3 = jnp.asarray(tab3, BF16)
    msk = jnp.asarray(msk, F32)

    def col(group):
        return pl.BlockSpec((1, ts, w), lambda i, j, group=group: (i, j, group))

    return pl.pallas_call(
        functools.partial(_hgrn_kernel, chunk=HG_CHUNK, heads=heads, layer=layer),
        grid=(b, s // ts),
        in_specs=[col(0), col(1), col(2), col(3),
                  pl.BlockSpec((nl, w), lambda i, j: (0, 0)),
                  pl.BlockSpec((1, HEAD), lambda i, j: (0, 0)),
                  pl.BlockSpec(tab3.shape, lambda i, j: (0, 0)),
                  pl.BlockSpec(msk.shape, lambda i, j: (0, 0, 0))],
        out_specs=pl.BlockSpec((1, ts, w), lambda i, j: (i, j, 0)),
        out_shape=jax.ShapeDtypeStruct((b, s, w), BF16),
        scratch_shapes=[pltpu.VMEM((heads, HEAD, HEAD), F32)],
        compiler_params=pltpu.CompilerParams(
            dimension_semantics=("arbitrary", "arbitrary"), vmem_limit_bytes=VMEM_LIMIT),
        name="hgrn2",
    )(proj, proj, proj, proj, lb_logits, norm_w, tab3, msk)


def _attn_kernel(aq_ref, ak_ref, av_ref, qw_ref, kw_ref, slope_ref, o_ref, q_s, k_s, o_s, l_s, *, seq):
    assert DILATIONS == (1, 4, 16) and seq == 16 * N_BACK
    blk = N_BACK
    npat = len(DILATIONS)
    slope = slope_ref[0]
    qw = qw_ref[...] * (HEAD ** -0.5)
    kw = kw_ref[...]
    rows_n = 256

    def norm_body(i, carry):
        rows = pl.ds(pl.multiple_of(i * rows_n, rows_n), rows_n)
        aq = aq_ref[0, rows, :]
        q_s[rows, :] = aq * lax.rsqrt(jnp.mean(aq * aq, axis=-1, keepdims=True) + EPS) * qw
        ak = ak_ref[0, rows, :]
        k_s[rows, :] = ak * lax.rsqrt(jnp.mean(ak * ak, axis=-1, keepdims=True) + EPS) * kw
        return carry

    lax.fori_loop(0, seq // rows_n, norm_body, 0)

    qi = lax.broadcasted_iota(jnp.int32, (blk, 2 * blk), 0)
    kj = lax.broadcasted_iota(jnp.int32, (blk, 2 * blk), 1)
    steps = qi + blk - kj
    ok = (steps >= 0) & (steps <= N_BACK)
    stepsf = steps.astype(F32)

    def one_block(p, dil, base, bias, with_prev):
        qb = q_s[pl.ds(base, blk, stride=dil), :].astype(BF16)
        if with_prev:
            kb = k_s[pl.ds(base - blk * dil, 2 * blk, stride=dil), :].astype(BF16)
            vb = av_ref[0, pl.ds(base - blk * dil, 2 * blk, stride=dil), :].astype(BF16)
            sc = _dot_nt(qb, kb) + bias
        else:
            kb = k_s[pl.ds(base, blk, stride=dil), :].astype(BF16)
            vb = av_ref[0, pl.ds(base, blk, stride=dil), :].astype(BF16)
            sc = _dot_nt(qb, kb) + bias[:, blk:]
        m = jnp.max(sc, axis=-1, keepdims=True)
        pr = jnp.exp(sc - m)
        den = jnp.sum(pr, axis=-1, keepdims=True)
        o = _dot(pr.astype(BF16), vb) / den
        o_s[p, pl.ds(base, blk, stride=dil), :] = o
        l_s[p, pl.ds(base, blk, stride=dil), :] = jnp.broadcast_to(m + jnp.log(den), (blk, HEAD))

    def bias_for(dil):
        return jnp.where(ok, -slope[:, 0:1] * (stepsf * float(dil)), -jnp.inf)

    bias1, bias4, bias16 = [bias_for(d) for d in DILATIONS]
    one_block(0, 1, 0, bias1, False)
    for r in range(4):
        one_block(1, 4, r, bias4, False)

    def body1(i, carry):
        for u in range(5):
            one_block(0, 1, (1 + i * 5 + u) * blk, bias1, True)
        return carry

    lax.fori_loop(0, (seq // blk - 1) // 5, body1, 0)

    def body4(n, carry):
        for r in range(4):
            one_block(1, 4, r + n * (4 * blk), bias4, True)
        return carry

    lax.fori_loop(1, seq // (4 * blk), body4, 0)

    def body16(i, carry):
        for u in range(4):
            one_block(2, 16, i * 4 + u, bias16, False)
        return carry

    lax.fori_loop(0, 4, body16, 0)

    def merge_body(i, carry):
        rows = pl.ds(pl.multiple_of(i * rows_n, rows_n), rows_n)
        ls = [l_s[p, rows, :] for p in range(npat)]
        m = functools.reduce(jnp.maximum, ls)
        es = [jnp.exp(l - m) for l in ls]
        den = functools.reduce(lambda a, b: a + b, es)
        num = functools.reduce(lambda a, b: a + b, [e * o_s[p, rows, :] for p, e in enumerate(es)])
        o_ref[0, rows, :] = (num / den).astype(o_ref.dtype)
        return carry

    lax.fori_loop(0, seq // rows_n, merge_body, 0)


def _attention(proj, q_norm_w, k_norm_w, heads, col0):
    b, s, _ = proj.shape
    slopes = jnp.exp2(-8.0 * jnp.arange(1, heads + 1, dtype=F32) / heads)
    slopes = jnp.broadcast_to(slopes[:, None, None], (heads, 1, HEAD))

    def col(group):
        return pl.BlockSpec((1, s, HEAD), lambda i, j, group=group: (i, 0, col0 + group * heads + j))

    npat = len(DILATIONS)
    return pl.pallas_call(
        functools.partial(_attn_kernel, seq=s),
        grid=(b, heads),
        in_specs=[col(0), col(1), col(2),
                  pl.BlockSpec((1, HEAD), lambda i, j: (0, 0)),
                  pl.BlockSpec((1, HEAD), lambda i, j: (0, 0)),
                  pl.BlockSpec((1, 1, HEAD), lambda i, j: (j, 0, 0))],
        out_specs=pl.BlockSpec((1, s, HEAD), lambda i, j: (i, 0, j)),
        out_shape=jax.ShapeDtypeStruct((b, s, heads * HEAD), BF16),
        scratch_shapes=[pltpu.VMEM((s, HEAD), F32), pltpu.VMEM((s, HEAD), F32),
                        pltpu.VMEM((npat, s, HEAD), F32), pltpu.VMEM((npat, s, HEAD), F32)],
        compiler_params=pltpu.CompilerParams(vmem_limit_bytes=VMEM_LIMIT),
        name="dilated_attn",
    )(proj, proj, proj, q_norm_w, k_norm_w, slopes)


def _outproj_kernel(a_ref, b_ref, wa_ref, wb_ref, x_ref, gate_ref, o_ref, wa_s, wb_s):
    @pl.when(pl.program_id(1) == 0)
    def _():
        wa_s[...] = wa_ref[...].astype(BF16)
        wb_s[...] = wb_ref[...].astype(BF16)

    mix = _dot(a_ref[...], wa_s[...]) + _dot(b_ref[...], wb_s[...])
    o_ref[...] = x_ref[...] + gate_ref[0] * mix


def _out_projection(a, b, w, x, gate, seq, tm=1024, tn=1024):
    m, ka = a.shape
    kb = b.shape[1]
    n = w.shape[1]
    per_seq = seq // tm
    return pl.pallas_call(
        _outproj_kernel,
        grid=(n // tn, m // tm),
        in_specs=[pl.BlockSpec((tm, ka), lambda j, i: (i, 0)),
                  pl.BlockSpec((tm, kb), lambda j, i: (i, 0)),
                  pl.BlockSpec((ka, tn), lambda j, i: (0, j)),
                  pl.BlockSpec((kb, tn), lambda j, i: (ka // kb, j)),
                  pl.BlockSpec((tm, tn), lambda j, i: (i, j)),
                  pl.BlockSpec((1, 1, tn), lambda j, i: (i // per_seq, 0, j))],
        out_specs=pl.BlockSpec((tm, tn), lambda j, i: (i, j)),
        out_shape=jax.ShapeDtypeStruct((m, n), F32),
        scratch_shapes=[pltpu.VMEM((ka, tn), BF16), pltpu.VMEM((kb, tn), BF16)],
        compiler_params=pltpu.CompilerParams(
            dimension_semantics=("arbitrary", "arbitrary"), vmem_limit_bytes=VMEM_LIMIT),
        name="out_proj",
    )(a, b, w, w, x, gate)


def _up_kernel(h_ref, wa_ref, wg_ref, cw_ref, cb_ref, y_ref, wa_s, wg_s, halo_s, *, per_seq):
    i = pl.program_id(1)

    @pl.when(i == 0)
    def _():
        wa_s[...] = wa_ref[...].astype(BF16)
        wg_s[...] = wg_ref[...].astype(BF16)

    @pl.when(i % per_seq == 0)
    def _():
        halo_s[...] = jnp.zeros_like(halo_s)

    h = h_ref[...]
    a = _dot(h, wa_s[...])
    g = _dot(h, wg_s[...])
    tm = a.shape[0]
    prev = halo_s[...]
    p1 = prev[7:8, :]
    p2 = prev[6:7, :]
    row = lax.broadcasted_iota(jnp.int32, a.shape, 0)
    a1 = jnp.where(row == 0, p1, pltpu.roll(a, 1, axis=0))
    a2 = jnp.where(row == 0, p2, jnp.where(row == 1, p1, pltpu.roll(a, 2, axis=0)))
    halo_s[...] = a[tm - 8:tm, :]
    cw = cw_ref[...]
    y = cb_ref[...] + a2 * cw[0:1, :] + a1 * cw[1:2, :] + a * cw[2:3, :]
    y_ref[...] = (_silu(y) * g).astype(y_ref.dtype)


def _up_projection(h, w, conv_w, conv_b, seq, tm=1024, tn=512):
    m, k = h.shape
    dff = w.shape[1] // 2
    nb = dff // tn
    return pl.pallas_call(
        functools.partial(_up_kernel, per_seq=seq // tm),
        grid=(nb, m // tm),
        in_specs=[pl.BlockSpec((tm, k), lambda j, i: (i, 0)),
                  pl.BlockSpec((k, tn), lambda j, i: (0, j)),
                  pl.BlockSpec((k, tn), lambda j, i: (0, nb + j)),
                  pl.BlockSpec((CONV_WIDTH, tn), lambda j, i: (0, j)),
                  pl.BlockSpec((1, tn), lambda j, i: (0, j))],
        out_specs=pl.BlockSpec((tm, tn), lambda j, i: (i, j)),
        out_shape=jax.ShapeDtypeStruct((m, dff), BF16),
        scratch_shapes=[pltpu.VMEM((k, tn), BF16), pltpu.VMEM((k, tn), BF16), pltpu.VMEM((8, tn), F32)],
        compiler_params=pltpu.CompilerParams(
            dimension_semantics=("arbitrary", "arbitrary"), vmem_limit_bytes=VMEM_LIMIT),
        name="up_proj_conv_gate",
    )(h, w, w, conv_w, conv_b)


def _down_kernel(y_ref, w_ref, x_ref, gate_ref, o_ref, w_s):
    @pl.when(pl.program_id(1) == 0)
    def _():
        w_s[...] = w_ref[...].astype(BF16)

    o_ref[...] = x_ref[...] + gate_ref[0] * _dot(y_ref[...], w_s[...])


def _down_projection(y, w, x, gate, seq, tm=512, tn=512):
    m, k = y.shape
    n = w.shape[1]
    per_seq = seq // tm
    return pl.pallas_call(
        _down_kernel,
        grid=(n // tn, m // tm),
        in_specs=[pl.BlockSpec((tm, k), lambda j, i: (i, 0)),
                  pl.BlockSpec((k, tn), lambda j, i: (0, j)),
                  pl.BlockSpec((tm, tn), lambda j, i: (i, j)),
                  pl.BlockSpec((1, 1, tn), lambda j, i: (i // per_seq, 0, j))],
        out_specs=pl.BlockSpec((tm, tn), lambda j, i: (i, j)),
        out_shape=jax.ShapeDtypeStruct((m, n), F32),
        scratch_shapes=[pltpu.VMEM((k, tn), BF16)],
        compiler_params=pltpu.CompilerParams(
            dimension_semantics=("arbitrary", "arbitrary"), vmem_limit_bytes=VMEM_LIMIT),
        name="down_proj",
    )(y, w, x, gate)


def _layer(layer, x, mod, norm1_w, w_in, lb_logits, hg_norm_w, q_norm_w, k_norm_w, w_out, norm2_w, w_up, conv_w,
           conv_b, w_down):
    b, s, d = x.shape
    shift1, scale1, gate1, shift2, scale2, gate2 = [mod[:b, j * d:(j + 1) * d].reshape(b, 1, d) for j in range(6)]

    h = _norm_modulate(x, norm1_w.reshape(1, d), scale1, shift1)
    proj = _in_projection(h.reshape(b * s, d), w_in).reshape(b, s, -1)
    a_out = _hgrn(proj, lb_logits, hg_norm_w.reshape(1, HEAD), HG_HEADS, layer)
    b_out = _attention(proj, q_norm_w.reshape(1, HEAD), k_norm_w.reshape(1, HEAD), ATT_HEADS, 4 * HG_HEADS)
    x1 = _out_projection(a_out.reshape(b * s, -1), b_out.reshape(b * s, -1), w_out, x.reshape(b * s, d), gate1, s)

    h2 = _norm_modulate(x1.reshape(b, s, d), norm2_w.reshape(1, d), scale2, shift2)
    y = _up_projection(h2.reshape(b * s, d), w_up, conv_w, conv_b.reshape(1, -1), s)
    x2 = _down_projection(y, w_down, x1, gate2, s)
    return x2.reshape(b, s, d)


def kernel(x, c, w_ada, b_ada, norm1_w, w_in, lb_logits, hg_norm_w, q_norm_w, k_norm_w, w_out, norm2_w, w_up, conv_w, conv_b, w_down):
    depth = w_ada.shape[0]
    b = x.shape[0]
    c_pad = jnp.pad(c, ((0, 8 - b), (0, 0)))
    for l in range(depth):
        mod = _modulation(c_pad, w_ada[l], b_ada[l].reshape(1, -1))
        x = _layer(l, x, mod, norm1_w[l], w_in[l], lb_logits, hg_norm_w[l], q_norm_w[l], k_norm_w[l],
                   w_out[l], norm2_w[l], w_up[l], conv_w[l], conv_b[l], w_down[l])
    return x
```

```python
import functools

import numpy as np
import jax
import jax.numpy as jnp
from jax import lax
from jax.experimental import pallas as pl
from jax.experimental.pallas import tpu as pltpu

F32 = jnp.float32
BF16 = jnp.bfloat16
EPS = 1e-6

HEAD = 128
HG_HEADS = 8
ATT_HEADS = 8
N_BACK = 128
DILATIONS = (1, 4, 16)
CONV_WIDTH = 3
HG_CHUNK = 64
VMEM_LIMIT = 56 * 1024 * 1024


def _dot(a, b):
    return jnp.dot(a, b, preferred_element_type=F32)


def _dot_nt(a, b):
    return lax.dot_general(a, b, (((1,), (1,)), ((), ())), preferred_element_type=F32)


def _dot_tn(a, b):
    return lax.dot_general(a, b, (((0,), (0,)), ((), ())), preferred_element_type=F32)


def _silu(v):
    return v * jax.nn.sigmoid(v)


def _mod_kernel(c_ref, w_ref, b_ref, o_ref):
    ca = _silu(c_ref[...]).astype(BF16)
    o_ref[...] = _dot(ca, w_ref[...].astype(BF16)) + b_ref[...]


def _modulation(c_pad, w_ada, b_ada, tn=1024):
    rows, d = c_pad.shape
    n = w_ada.shape[1]
    return pl.pallas_call(
        _mod_kernel,
        grid=(n // tn,),
        in_specs=[pl.BlockSpec((rows, d), lambda j: (0, 0)),
                  pl.BlockSpec((d, tn), lambda j: (0, j)),
                  pl.BlockSpec((1, tn), lambda j: (0, j))],
        out_specs=pl.BlockSpec((rows, tn), lambda j: (0, j)),
        out_shape=jax.ShapeDtypeStruct((rows, n), F32),
        compiler_params=pltpu.CompilerParams(vmem_limit_bytes=VMEM_LIMIT),
        name="adaln_mod",
    )(c_pad, w_ada, b_ada)


def _norm_kernel(x_ref, w_ref, scale_ref, shift_ref, o_ref):
    x = x_ref[0]
    ms = jnp.mean(x * x, axis=-1, keepdims=True)
    y = x * lax.rsqrt(ms + EPS) * w_ref[...]
    o_ref[0] = (y * (1.0 + scale_ref[0]) + shift_ref[0]).astype(o_ref.dtype)


def _norm_modulate(x, w, scale, shift, ts=512):
    b, s, d = x.shape
    return pl.pallas_call(
        _norm_kernel,
        grid=(b, s // ts),
        in_specs=[pl.BlockSpec((1, ts, d), lambda i, j: (i, j, 0)),
                  pl.BlockSpec((1, d), lambda i, j: (0, 0)),
                  pl.BlockSpec((1, 1, d), lambda i, j: (i, 0, 0)),
                  pl.BlockSpec((1, 1, d), lambda i, j: (i, 0, 0))],
        out_specs=pl.BlockSpec((1, ts, d), lambda i, j: (i, j, 0)),
        out_shape=jax.ShapeDtypeStruct((b, s, d), BF16),
        compiler_params=pltpu.CompilerParams(vmem_limit_bytes=VMEM_LIMIT),
        name="norm_modulate",
    )(x, w, scale, shift)


def _inproj_kernel(x_ref, w_ref, o_ref, wb_ref):
    @pl.when(pl.program_id(1) == 0)
    def _():
        wb_ref[...] = w_ref[...].astype(BF16)

    o_ref[...] = _dot(x_ref[...], wb_ref[...])


def _in_projection(h, w, tm=1024, tn=1024):
    m, k = h.shape
    n = w.shape[1]
    return pl.pallas_call(
        _inproj_kernel,
        grid=(n // tn, m // tm),
        in_specs=[pl.BlockSpec((tm, k), lambda j, i: (i, 0)),
                  pl.BlockSpec((k, tn), lambda j, i: (0, j))],
        out_specs=pl.BlockSpec((tm, tn), lambda j, i: (i, j)),
        out_shape=jax.ShapeDtypeStruct((m, n), F32),
        scratch_shapes=[pltpu.VMEM((k, tn), BF16)],
        compiler_params=pltpu.CompilerParams(
            dimension_semantics=("arbitrary", "arbitrary"), vmem_limit_bytes=VMEM_LIMIT),
        name="in_proj",
    )(h, w)


def _hgrn_tables(c):
    t = np.arange(c)[:, None]
    u = np.arange(c)[None, :]
    tables = [(u <= t), (u > t)]
    masks = []
    m = c
    while m >= 2:
        half = m // 2
        start = (t // m) * m
        mid = start + half - 1
        upper = (t - start) >= half
        tables.append(np.where(upper, (u > mid) & (u <= t), (u > t) & (u <= mid)))
        u_start = (u // m) * m
        masks.append(upper & ((u - u_start) < half) & (u_start == start))
        m //= 2
    tab = np.concatenate(tables, axis=0).astype(np.float32)
    tab3 = np.concatenate([tab, tab, tab], axis=1)
    msk = np.stack(masks).astype(np.float32)
    return tab3, msk


def _hgrn_kernel(hq_ref, hf_ref, hi_ref, hg_ref, lbl_ref, nw_ref, tab_ref, msk_ref, o_ref, st_ref, *,
                 chunk, heads, layer):
    c = chunk
    levels = msk_ref.shape[0]
    ts = hq_ref.shape[1]

    @pl.when(pl.program_id(1) == 0)
    def _():
        st_ref[...] = jnp.zeros_like(st_ref)

    lbl = lbl_ref[...]
    ex = jnp.exp(lbl - jnp.max(lbl, axis=0, keepdims=True))
    lb_all = jnp.sum(ex[0:layer + 1], axis=0, keepdims=True) / jnp.sum(ex, axis=0, keepdims=True)
    nw = nw_ref[...]
    eye = (lax.broadcasted_iota(jnp.int32, (c, c), 0) == lax.broadcasted_iota(jnp.int32, (c, c), 1))

    def one_head(rows, h):
        cols = slice(h * HEAD, (h + 1) * HEAD)
        lb = lb_all[:, cols]
        q = _silu(hq_ref[0, rows, cols])
        f = lb + (1.0 - lb) * jax.nn.sigmoid(hf_ref[0, rows, cols])
        g = jnp.log(f)
        kk = 1.0 - f
        v = hi_ref[0, rows, cols].astype(BF16)

        g1 = g.astype(BF16)
        r1 = g - g1.astype(F32)
        g2 = r1.astype(BF16)
        g3 = (r1 - g2.astype(F32)).astype(BF16)
        sums = _dot(tab_ref[...], jnp.concatenate([g1, g2, g3], axis=0))
        dec = jnp.exp(sums)

        state_t = st_ref[h]
        e_cum = dec[0:c]
        o = _dot_nt((q * e_cum).astype(BF16), state_t.astype(BF16))

        a = jnp.where(eye, _dot_nt(q.astype(BF16), kk.astype(BF16)), 0.0)
        for l in range(levels):
            e = dec[(2 + l) * c:(3 + l) * c]
            sc = _dot_nt((q * e).astype(BF16), (kk * e).astype(BF16))
            a = jnp.where(msk_ref[l] > 0.5, sc, a)
        o = o + _dot(a.astype(BF16), v)

        k_dec = (kk * dec[c:2 * c]).astype(BF16)
        st_ref[h] = state_t * e_cum[c - 1:c, :] + _dot_tn(v, k_dec)

        ms = jnp.mean(o * o, axis=-1, keepdims=True)
        on = o * lax.rsqrt(ms + EPS) * nw
        o_ref[0, rows, cols] = (on * _silu(hg_ref[0, rows, cols])).astype(o_ref.dtype)

    def body(i, carry):
        rows = pl.ds(pl.multiple_of(i * c, c), c)
        for h in range(heads):
            one_head(rows, h)
        return carry

    lax.fori_loop(0, ts // c, body, 0, unroll=2)


def _hgrn(proj, lb_logits, norm_w, heads, layer, ts=512):
    b, s, _ = proj.shape
    nl = lb_logits.shape[0]
    w = heads * HEAD
    tab3, msk = _hgrn_tables(HG_CHUNK)
    tab3 = jnp.asarray(tab3, BF16)
    msk = jnp.asarray(msk, F32)

    def col(group):
        return pl.BlockSpec((1, ts, w), lambda i, j, group=group: (i, j, group))

    return pl.pallas_call(
        functools.partial(_hgrn_kernel, chunk=HG_CHUNK, heads=heads, layer=layer),
        grid=(b, s // ts),
        in_specs=[col(0), col(1), col(2), col(3),
                  pl.BlockSpec((nl, w), lambda i, j: (0, 0)),
                  pl.BlockSpec((1, HEAD), lambda i, j: (0, 0)),
                  pl.BlockSpec(tab3.shape, lambda i, j: (0, 0)),
                  pl.BlockSpec(msk.shape, lambda i, j: (0, 0, 0))],
        out_specs=pl.BlockSpec((1, ts, w), lambda i, j: (i, j, 0)),
        out_shape=jax.ShapeDtypeStruct((b, s, w), BF16),
        scratch_shapes=[pltpu.VMEM((heads, HEAD, HEAD), F32)],
        compiler_params=pltpu.CompilerParams(
            dimension_semantics=("arbitrary", "arbitrary"), vmem_limit_bytes=VMEM_LIMIT),
        name="hgrn2",
    )(proj, proj, proj, proj, lb_logits, norm_w, tab3, msk)


def _attn_kernel(aq_ref, ak_ref, av_ref, qw_ref, kw_ref, slope_ref, o_ref, q_s, k_s, o_s, m_s, d_s, *, seq):
    assert DILATIONS == (1, 4, 16) and seq == 16 * N_BACK
    blk = N_BACK
    npat = len(DILATIONS)
    slope = slope_ref[0]
    qw = qw_ref[...] * (HEAD ** -0.5)
    kw = kw_ref[...]
    rows_n = 256

    def norm_body(i, carry):
        rows = pl.ds(pl.multiple_of(i * rows_n, rows_n), rows_n)
        aq = aq_ref[0, rows, :]
        q_s[rows, :] = aq * lax.rsqrt(jnp.mean(aq * aq, axis=-1, keepdims=True) + EPS) * qw
        ak = ak_ref[0, rows, :]
        k_s[rows, :] = ak * lax.rsqrt(jnp.mean(ak * ak, axis=-1, keepdims=True) + EPS) * kw
        return carry

    lax.fori_loop(0, seq // rows_n, norm_body, 0)

    qi = lax.broadcasted_iota(jnp.int32, (blk, 2 * blk), 0)
    kj = lax.broadcasted_iota(jnp.int32, (blk, 2 * blk), 1)
    steps = qi + blk - kj
    ok = (steps >= 0) & (steps <= N_BACK)
    stepsf = steps.astype(F32)

    def one_block(p, dil, base, bias, with_prev):
        qb = q_s[pl.ds(base, blk, stride=dil), :].astype(BF16)
        if with_prev:
            kb = k_s[pl.ds(base - blk * dil, 2 * blk, stride=dil), :].astype(BF16)
            vb = av_ref[0, pl.ds(base - blk * dil, 2 * blk, stride=dil), :].astype(BF16)
            sc = _dot_nt(qb, kb) + bias
        else:
            kb = k_s[pl.ds(base, blk, stride=dil), :].astype(BF16)
            vb = av_ref[0, pl.ds(base, blk, stride=dil), :].astype(BF16)
            sc = _dot_nt(qb, kb) + bias[:, blk:]
        m = jnp.max(sc, axis=-1, keepdims=True)
        pr = jnp.exp(sc - m).astype(BF16)
        res = _dot(pr, jnp.concatenate([vb, jnp.ones_like(vb)], axis=1))
        o_s[p, pl.ds(base, blk, stride=dil), :] = res[:, :HEAD]
        d_s[p, pl.ds(base, blk, stride=dil), :] = res[:, HEAD:]
        m_s[p, pl.ds(base, blk, stride=dil), :] = jnp.broadcast_to(m, (blk, HEAD))

    def bias_for(dil):
        return jnp.where(ok, -slope[:, 0:1] * (stepsf * float(dil)), -jnp.inf)

    bias1, bias4, bias16 = [bias_for(d) for d in DILATIONS]
    one_block(0, 1, 0, bias1, False)
    for r in range(4):
        one_block(1, 4, r, bias4, False)

    for n in range(1, seq // blk):
        one_block(0, 1, n * blk, bias1, True)
    for n in range(1, seq // (4 * blk)):
        for r in range(4):
            one_block(1, 4, r + n * (4 * blk), bias4, True)
    for r in range(16):
        one_block(2, 16, r, bias16, False)

    def merge_body(i, carry):
        rows = pl.ds(pl.multiple_of(i * rows_n, rows_n), rows_n)
        ms = [m_s[p, rows, :] for p in range(npat)]
        m = functools.reduce(jnp.maximum, ms)
        es = [jnp.exp(mp - m) for mp in ms]
        den = functools.reduce(lambda a, b: a + b, [e * d_s[p, rows, :] for p, e in enumerate(es)])
        num = functools.reduce(lambda a, b: a + b, [e * o_s[p, rows, :] for p, e in enumerate(es)])
        o_ref[0, rows, :] = (num / den).astype(o_ref.dtype)
        return carry

    lax.fori_loop(0, seq // rows_n, merge_body, 0)


def _attention(proj, q_norm_w, k_norm_w, heads, col0):
    b, s, _ = proj.shape
    slopes = jnp.exp2(-8.0 * jnp.arange(1, heads + 1, dtype=F32) / heads)
    slopes = jnp.broadcast_to(slopes[:, None, None], (heads, 1, HEAD))

    def col(group):
        return pl.BlockSpec((1, s, HEAD), lambda i, j, group=group: (i, 0, col0 + group * heads + j))

    npat = len(DILATIONS)
    return pl.pallas_call(
        functools.partial(_attn_kernel, seq=s),
        grid=(b, heads),
        in_specs=[col(0), col(1), col(2),
                  pl.BlockSpec((1, HEAD), lambda i, j: (0, 0)),
                  pl.BlockSpec((1, HEAD), lambda i, j: (0, 0)),
                  pl.BlockSpec((1, 1, HEAD), lambda i, j: (j, 0, 0))],
        out_specs=pl.BlockSpec((1, s, HEAD), lambda i, j: (i, 0, j)),
        out_shape=jax.ShapeDtypeStruct((b, s, heads * HEAD), BF16),
        scratch_shapes=[pltpu.VMEM((s, HEAD), F32), pltpu.VMEM((s, HEAD), F32),
                        pltpu.VMEM((npat, s, HEAD), F32), pltpu.VMEM((npat, s, HEAD), F32),
                        pltpu.VMEM((npat, s, HEAD), F32)],
        compiler_params=pltpu.CompilerParams(vmem_limit_bytes=VMEM_LIMIT),
        name="dilated_attn",
    )(proj, proj, proj, q_norm_w, k_norm_w, slopes)


def _outproj_kernel(a_ref, b_ref, wa_ref, wb_ref, x_ref, gate_ref, o_ref, wa_s, wb_s):
    @pl.when(pl.program_id(1) == 0)
    def _():
        wa_s[...] = wa_ref[...].astype(BF16)
        wb_s[...] = wb_ref[...].astype(BF16)

    mix = _dot(a_ref[...], wa_s[...]) + _dot(b_ref[...], wb_s[...])
    o_ref[...] = x_ref[...] + gate_ref[0] * mix


def _out_projection(a, b, w, x, gate, seq, tm=1024, tn=1024):
    m, ka = a.shape
    kb = b.shape[1]
    n = w.shape[1]
    per_seq = seq // tm
    return pl.pallas_call(
        _outproj_kernel,
        grid=(n // tn, m // tm),
        in_specs=[pl.BlockSpec((tm, ka), lambda j, i: (i, 0)),
                  pl.BlockSpec((tm, kb), lambda j, i: (i, 0)),
                  pl.BlockSpec((ka, tn), lambda j, i: (0, j)),
                  pl.BlockSpec((kb, tn), lambda j, i: (ka // kb, j)),
                  pl.BlockSpec((tm, tn), lambda j, i: (i, j)),
                  pl.BlockSpec((1, 1, tn), lambda j, i: (i // per_seq, 0, j))],
        out_specs=pl.BlockSpec((tm, tn), lambda j, i: (i, j)),
        out_shape=jax.ShapeDtypeStruct((m, n), F32),
        scratch_shapes=[pltpu.VMEM((ka, tn), BF16), pltpu.VMEM((kb, tn), BF16)],
        compiler_params=pltpu.CompilerParams(
            dimension_semantics=("arbitrary", "arbitrary"), vmem_limit_bytes=VMEM_LIMIT),
        name="out_proj",
    )(a, b, w, w, x, gate)


def _up_kernel(h_ref, wa_ref, wg_ref, cw_ref, cb_ref, y_ref, wa_s, wg_s, halo_s, *, per_seq):
    i = pl.program_id(1)

    @pl.when(i == 0)
    def _():
        wa_s[...] = wa_ref[...].astype(BF16)
        wg_s[...] = wg_ref[...].astype(BF16)

    @pl.when(i % per_seq == 0)
    def _():
        halo_s[...] = jnp.zeros_like(halo_s)

    h = h_ref[...]
    a = _dot(h, wa_s[...])
    g = _dot(h, wg_s[...])
    tm = a.shape[0]
    prev = halo_s[...]
    p1 = prev[7:8, :]
    p2 = prev[6:7, :]
    row = lax.broadcasted_iota(jnp.int32, a.shape, 0)
    a1 = jnp.where(row == 0, p1, pltpu.roll(a, 1, axis=0))
    a2 = jnp.where(row == 0, p2, jnp.where(row == 1, p1, pltpu.roll(a, 2, axis=0)))
    halo_s[...] = a[tm - 8:tm, :]
    cw = cw_ref[...]
    y = cb_ref[...] + a2 * cw[0:1, :] + a1 * cw[1:2, :] + a * cw[2:3, :]
    y_ref[...] = (_silu(y) * g).astype(y_ref.dtype)


def _up_projection(h, w, conv_w, conv_b, seq, tm=1024, tn=512):
    m, k = h.shape
    dff = w.shape[1] // 2
    nb = dff // tn
    return pl.pallas_call(
        functools.partial(_up_kernel, per_seq=seq // tm),
        grid=(nb, m // tm),
        in_specs=[pl.BlockSpec((tm, k), lambda j, i: (i, 0)),
                  pl.BlockSpec((k, tn), lambda j, i: (0, j)),
                  pl.BlockSpec((k, tn), lambda j, i: (0, nb + j)),
                  pl.BlockSpec((CONV_WIDTH, tn), lambda j, i: (0, j)),
                  pl.BlockSpec((1, tn), lambda j, i: (0, j))],
        out_specs=pl.BlockSpec((tm, tn), lambda j, i: (i, j)),
        out_shape=jax.ShapeDtypeStruct((m, dff), BF16),
        scratch_shapes=[pltpu.VMEM((k, tn), BF16), pltpu.VMEM((k, tn), BF16), pltpu.VMEM((8, tn), F32)],
        compiler_params=pltpu.CompilerParams(
            dimension_semantics=("arbitrary", "arbitrary"), vmem_limit_bytes=VMEM_LIMIT),
        name="up_proj_conv_gate",
    )(h, w, w, conv_w, conv_b)


def _down_kernel(y_ref, w_ref, x_ref, gate_ref, o_ref, w_s):
    @pl.when(pl.program_id(1) == 0)
    def _():
        w_s[...] = w_ref[...].astype(BF16)

    o_ref[...] = x_ref[...] + gate_ref[0] * _dot(y_ref[...], w_s[...])


def _down_projection(y, w, x, gate, seq, tm=512, tn=512):
    m, k = y.shape
    n = w.shape[1]
    per_seq = seq // tm
    return pl.pallas_call(
        _down_kernel,
        grid=(n // tn, m // tm),
        in_specs=[pl.BlockSpec((tm, k), lambda j, i: (i, 0)),
                  pl.BlockSpec((k, tn), lambda j, i: (0, j)),
                  pl.BlockSpec((tm, tn), lambda j, i: (i, j)),
                  pl.BlockSpec((1, 1, tn), lambda j, i: (i // per_seq, 0, j))],
        out_specs=pl.BlockSpec((tm, tn), lambda j, i: (i, j)),
        out_shape=jax.ShapeDtypeStruct((m, n), F32),
        scratch_shapes=[pltpu.VMEM((k, tn), BF16)],
        compiler_params=pltpu.CompilerParams(
            dimension_semantics=("arbitrary", "arbitrary"), vmem_limit_bytes=VMEM_LIMIT),
        name="down_proj",
    )(y, w, x, gate)


def _layer(layer, x, mod, norm1_w, w_in, lb_logits, hg_norm_w, q_norm_w, k_norm_w, w_out, norm2_w, w_up, conv_w,
           conv_b, w_down):
    b, s, d = x.shape
    shift1, scale1, gate1, shift2, scale2, gate2 = [mod[:b, j * d:(j + 1) * d].reshape(b, 1, d) for j in range(6)]

    h = _norm_modulate(x, norm1_w.reshape(1, d), scale1, shift1)
    proj = _in_projection(h.reshape(b * s, d), w_in).reshape(b, s, -1)
    a_out = _hgrn(proj, lb_logits, hg_norm_w.reshape(1, HEAD), HG_HEADS, layer)
    b_out = _attention(proj, q_norm_w.reshape(1, HEAD), k_norm_w.reshape(1, HEAD), ATT_HEADS, 4 * HG_HEADS)
    x1 = _out_projection(a_out.reshape(b * s, -1), b_out.reshape(b * s, -1), w_out, x.reshape(b * s, d), gate1, s)

    h2 = _norm_modulate(x1.reshape(b, s, d), norm2_w.reshape(1, d), scale2, shift2)
    y = _up_projection(h2.reshape(b * s, d), w_up, conv_w, conv_b.reshape(1, -1), s)
    x2 = _down_projection(y, w_down, x1, gate2, s)
    return x2.reshape(b, s, d)


def kernel(x, c, w_ada, b_ada, norm1_w, w_in, lb_logits, hg_norm_w, q_norm_w, k_norm_w, w_out, norm2_w, w_up, conv_w, conv_b, w_down):
    depth = w_ada.shape[0]
    b = x.shape[0]
    c_pad = jnp.pad(c, ((0, 8 - b), (0, 0)))
    for l in range(depth):
        mod = _modulation(c_pad, w_ada[l], b_ada[l].reshape(1, -1))
        x = _layer(l, x, mod, norm1_w[l], w_in[l], lb_logits, hg_norm_w[l], q_norm_w[l], k_norm_w[l],
                   w_out[l], norm2_w[l], w_up[l], conv_w[l], conv_b[l], w_down[l])
    return x
```

```python
import functools

import numpy as np
import jax
import jax.numpy as jnp
from jax import lax
from jax.experimental import pallas as pl
from jax.experimental.pallas import tpu as pltpu

F32 = jnp.float32
BF16 = jnp.bfloat16
EPS = 1e-6
LOG2E = 1.4426950408889634

HEAD = 128
HG_HEADS = 8
ATT_HEADS = 8
N_BACK = 128
DILATIONS = (1, 4, 16)
CONV_WIDTH = 3
HG_CHUNK = 64
VMEM_LIMIT = 56 * 1024 * 1024


def _dot(a, b):
    return jnp.dot(a, b, preferred_element_type=F32)


def _dot_nt(a, b):
    return lax.dot_general(a, b, (((1,), (1,)), ((), ())), preferred_element_type=F32)


def _dot_tn(a, b):
    return lax.dot_general(a, b, (((0,), (0,)), ((), ())), preferred_element_type=F32)


def _silu(v):
    return v * jax.nn.sigmoid(v)


def _mod_kernel(c_ref, w_ref, b_ref, o_ref):
    ca = _silu(c_ref[...]).astype(BF16)
    o_ref[...] = _dot(ca, w_ref[...].astype(BF16)) + b_ref[...]


def _modulation(c_pad, w_ada, b_ada, tn=1024):
    rows, d = c_pad.shape
    n = w_ada.shape[1]
    return pl.pallas_call(
        _mod_kernel,
        grid=(n // tn,),
        in_specs=[pl.BlockSpec((rows, d), lambda j: (0, 0)),
                  pl.BlockSpec((d, tn), lambda j: (0, j)),
                  pl.BlockSpec((1, tn), lambda j: (0, j))],
        out_specs=pl.BlockSpec((rows, tn), lambda j: (0, j)),
        out_shape=jax.ShapeDtypeStruct((rows, n), F32),
        compiler_params=pltpu.CompilerParams(vmem_limit_bytes=VMEM_LIMIT),
        name="adaln_mod",
    )(c_pad, w_ada, b_ada)


def _norm_kernel(x_ref, w_ref, scale_ref, shift_ref, o_ref):
    x = x_ref[0]
    ms = jnp.mean(x * x, axis=-1, keepdims=True)
    y = x * lax.rsqrt(ms + EPS) * w_ref[...]
    o_ref[0] = (y * (1.0 + scale_ref[0]) + shift_ref[0]).astype(o_ref.dtype)


def _norm_modulate(x, w, scale, shift, ts=512):
    b, s, d = x.shape
    return pl.pallas_call(
        _norm_kernel,
        grid=(b, s // ts),
        in_specs=[pl.BlockSpec((1, ts, d), lambda i, j: (i, j, 0)),
                  pl.BlockSpec((1, d), lambda i, j: (0, 0)),
                  pl.BlockSpec((1, 1, d), lambda i, j: (i, 0, 0)),
                  pl.BlockSpec((1, 1, d), lambda i, j: (i, 0, 0))],
        out_specs=pl.BlockSpec((1, ts, d), lambda i, j: (i, j, 0)),
        out_shape=jax.ShapeDtypeStruct((b, s, d), BF16),
        compiler_params=pltpu.CompilerParams(vmem_limit_bytes=VMEM_LIMIT),
        name="norm_modulate",
    )(x, w, scale, shift)


def _inproj_kernel(x_ref, w_ref, o_ref, wb_ref):
    @pl.when(pl.program_id(1) == 0)
    def _():
        wb_ref[...] = w_ref[...].astype(BF16)

    o_ref[...] = _dot(x_ref[...], wb_ref[...])


def _in_projection(h, w, tm=1024, tn=1024):
    m, k = h.shape
    n = w.shape[1]
    return pl.pallas_call(
        _inproj_kernel,
        grid=(n // tn, m // tm),
        in_specs=[pl.BlockSpec((tm, k), lambda j, i: (i, 0)),
                  pl.BlockSpec((k, tn), lambda j, i: (0, j))],
        out_specs=pl.BlockSpec((tm, tn), lambda j, i: (i, j)),
        out_shape=jax.ShapeDtypeStruct((m, n), F32),
        scratch_shapes=[pltpu.VMEM((k, tn), BF16)],
        compiler_params=pltpu.CompilerParams(
            dimension_semantics=("arbitrary", "arbitrary"), vmem_limit_bytes=VMEM_LIMIT),
        name="in_proj",
    )(h, w)


def _hgrn_tables(c):
    t = np.arange(c)[:, None]
    u = np.arange(c)[None, :]
    masks = []
    m = c
    while m >= 2:
        half = m // 2
        start = (t // m) * m
        u_start = (u // m) * m
        masks.append(((t - start) >= half) & ((u - u_start) < half) & (u_start == start))
        m //= 2
    cum = (u <= t).astype(np.float32)
    return np.concatenate([cum, cum, cum], axis=1), np.stack(masks).astype(np.float32)


def _level_signs(c, width):
    t = lax.broadcasted_iota(jnp.int32, (c, width), 0)
    signs = []
    m = c
    while m >= 8:
        signs.append(jnp.where(t % m >= m // 2, 1.0, -1.0).astype(F32))
        m //= 2
    return signs


def _level_log_decays(g, cum, signs):
    c = g.shape[0]
    out = []
    m = c
    for sgn in signs:
        half = m // 2
        mids = [jnp.broadcast_to(cum[j * m + half - 1:j * m + half, :], (m, g.shape[1])) for j in range(c // m)]
        out.append((cum - jnp.concatenate(mids, axis=0)) * sgn)
        m //= 2
    pos = lax.broadcasted_iota(jnp.int32, g.shape, 0) % 4
    g_next = pltpu.roll(g, c - 1, axis=0)
    g_prev = pltpu.roll(g, 1, axis=0)
    out.append(jnp.where(pos == 0, g_next, jnp.where(pos == 1, 0.0, jnp.where(pos == 2, g, g + g_prev))))
    out.append(jnp.where(pos % 2 == 1, g, 0.0))
    return out


def _hgrn_kernel(hq_ref, hf_ref, hi_ref, hg_ref, lbl_ref, nw_ref, tab_ref, msk_ref, o_ref, st_ref, *,
                 chunk, heads, layer):
    c = chunk
    assert chunk >= 8 and 2 ** msk_ref.shape[0] == chunk
    ts = hq_ref.shape[1]

    @pl.when(pl.program_id(1) == 0)
    def _():
        st_ref[...] = jnp.zeros_like(st_ref)

    lbl = lbl_ref[...]
    ex = jnp.exp(lbl - jnp.max(lbl, axis=0, keepdims=True))
    lb_all = jnp.sum(ex[0:layer + 1], axis=0, keepdims=True) / jnp.sum(ex, axis=0, keepdims=True)
    nw = nw_ref[...]
    eye = (lax.broadcasted_iota(jnp.int32, (c, c), 0) == lax.broadcasted_iota(jnp.int32, (c, c), 1))
    signs = _level_signs(c, HEAD)

    def one_head(rows, h):
        cols = slice(h * HEAD, (h + 1) * HEAD)
        lb = lb_all[:, cols]
        q = _silu(hq_ref[0, rows, cols])
        f = lb + (1.0 - lb) * jax.nn.sigmoid(hf_ref[0, rows, cols])
        g = jnp.log(f) * LOG2E
        kk = 1.0 - f
        v = hi_ref[0, rows, cols].astype(BF16)

        g1 = g.astype(BF16)
        r1 = g - g1.astype(F32)
        g2 = r1.astype(BF16)
        g3 = (r1 - g2.astype(F32)).astype(BF16)
        cum = _dot(tab_ref[...], jnp.concatenate([g1, g2, g3], axis=0))
        e_cum = jnp.exp2(cum)
        e_last = jnp.exp2(cum[c - 1:c, :] - cum)

        state_t = st_ref[h]
        o = _dot_nt((q * e_cum).astype(BF16), state_t.astype(BF16))

        a = jnp.where(eye, _dot_nt(q.astype(BF16), kk.astype(BF16)), 0.0)
        for l, d in enumerate(_level_log_decays(g, cum, signs)):
            e = jnp.exp2(d)
            sc = _dot_nt((q * e).astype(BF16), (kk * e).astype(BF16))
            a = jnp.where(msk_ref[l] > 0.5, sc, a)
        o = o + _dot(a.astype(BF16), v)

        k_dec = (kk * e_last).astype(BF16)
        st_ref[h] = state_t * e_cum[c - 1:c, :] + _dot_tn(v, k_dec)

        ms = jnp.mean(o * o, axis=-1, keepdims=True)
        on = o * lax.rsqrt(ms + EPS) * nw
        o_ref[0, rows, cols] = (on * _silu(hg_ref[0, rows, cols])).astype(o_ref.dtype)

    def body(i, carry):
        rows = pl.ds(pl.multiple_of(i * c, c), c)
        for h in range(heads):
            one_head(rows, h)
        return carry

    lax.fori_loop(0, ts // c, body, 0, unroll=2)


def _hgrn(proj, lb_logits, norm_w, heads, layer, ts=512):
    b, s, _ = proj.shape
    nl = lb_logits.shape[0]
    w = heads * HEAD
    tab3, msk = _hgrn_tables(HG_CHUNK)
    tab3 = jnp.asarray(tab3, BF16)
    msk = jnp.asarray(msk, F32)

    def col(group):
        return pl.BlockSpec((1, ts, w), lambda i, j, group=group: (i, j, group))

    return pl.pallas_call(
        functools.partial(_hgrn_kernel, chunk=HG_CHUNK, heads=heads, layer=layer),
        grid=(b, s // ts),
        in_specs=[col(0), col(1), col(2), col(3),
                  pl.BlockSpec((nl, w), lambda i, j: (0, 0)),
                  pl.BlockSpec((1, HEAD), lambda i, j: (0, 0)),
                  pl.BlockSpec(tab3.shape, lambda i, j: (0, 0)),
                  pl.BlockSpec(msk.shape, lambda i, j: (0, 0, 0))],
        out_specs=pl.BlockSpec((1, ts, w), lambda i, j: (i, j, 0)),
        out_shape=jax.ShapeDtypeStruct((b, s, w), BF16),
        scratch_shapes=[pltpu.VMEM((heads, HEAD, HEAD), F32)],
        compiler_params=pltpu.CompilerParams(
            dimension_semantics=("arbitrary", "arbitrary"), vmem_limit_bytes=VMEM_LIMIT),
        name="hgrn2",
    )(proj, proj, proj, proj, lb_logits, norm_w, tab3, msk)


def _attn_kernel(aq_ref, ak_ref, av_ref, qw_ref, kw_ref, slope_ref, o_ref, q_s, k_s, o_s, m_s, d_s, *, seq):
    assert DILATIONS == (1, 4, 16) and seq == 16 * N_BACK
    blk = N_BACK
    npat = len(DILATIONS)
    slope = slope_ref[0]
    qw = qw_ref[...] * (HEAD ** -0.5)
    kw = kw_ref[...]
    rows_n = 256

    def norm_body(i, carry):
        rows = pl.ds(pl.multiple_of(i * rows_n, rows_n), rows_n)
        aq = aq_ref[0, rows, :]
        q_s[rows, :] = aq * lax.rsqrt(jnp.mean(aq * aq, axis=-1, keepdims=True) + EPS) * qw
        ak = ak_ref[0, rows, :]
        k_s[rows, :] = ak * lax.rsqrt(jnp.mean(ak * ak, axis=-1, keepdims=True) + EPS) * kw
        return carry

    lax.fori_loop(0, seq // rows_n, norm_body, 0)

    qi = lax.broadcasted_iota(jnp.int32, (blk, 2 * blk), 0)
    kj = lax.broadcasted_iota(jnp.int32, (blk, 2 * blk), 1)
    steps = qi + blk - kj
    ok = (steps >= 0) & (steps <= N_BACK)
    stepsf = steps.astype(F32)

    def one_block(p, dil, base, bias, with_prev):
        qb = q_s[pl.ds(base, blk, stride=dil), :].astype(BF16)
        if with_prev:
            kb = k_s[pl.ds(base - blk * dil, 2 * blk, stride=dil), :].astype(BF16)
            vb = av_ref[0, pl.ds(base - blk * dil, 2 * blk, stride=dil), :].astype(BF16)
            sc = _dot_nt(qb, kb) + bias
        else:
            kb = k_s[pl.ds(base, blk, stride=dil), :].astype(BF16)
            vb = av_ref[0, pl.ds(base, blk, stride=dil), :].astype(BF16)
            sc = _dot_nt(qb, kb) + bias[:, blk:]
        m = jnp.max(sc, axis=-1, keepdims=True)
        pr = jnp.exp(sc - m).astype(BF16)
        res = _dot(pr, jnp.concatenate([vb, jnp.ones_like(vb)], axis=1))
        o_s[p, pl.ds(base, blk, stride=dil), :] = res[:, :HEAD]
        d_s[p, pl.ds(base, blk, stride=dil), :] = res[:, HEAD:]
        m_s[p, pl.ds(base, blk, stride=dil), :] = jnp.broadcast_to(m, (blk, HEAD))

    def bias_for(dil):
        return jnp.where(ok, -slope[:, 0:1] * (stepsf * float(dil)), -jnp.inf)

    bias1, bias4, bias16 = [bias_for(d) for d in DILATIONS]
    one_block(0, 1, 0, bias1, False)
    for r in range(4):
        one_block(1, 4, r, bias4, False)

    for n in range(1, seq // blk):
        one_block(0, 1, n * blk, bias1, True)
    for n in range(1, seq // (4 * blk)):
        for r in range(4):
            one_block(1, 4, r + n * (4 * blk), bias4, True)
    for r in range(16):
        one_block(2, 16, r, bias16, False)

    def merge_body(i, carry):
        rows = pl.ds(pl.multiple_of(i * rows_n, rows_n), rows_n)
        ms = [m_s[p, rows, :] for p in range(npat)]
        m = functools.reduce(jnp.maximum, ms)
        es = [jnp.exp(mp - m) for mp in ms]
        den = functools.reduce(lambda a, b: a + b, [e * d_s[p, rows, :] for p, e in enumerate(es)])
        num = functools.reduce(lambda a, b: a + b, [e * o_s[p, rows, :] for p, e in enumerate(es)])
        o_ref[0, rows, :] = (num / den).astype(o_ref.dtype)
        return carry

    lax.fori_loop(0, seq // rows_n, merge_body, 0)


def _attention(proj, q_norm_w, k_norm_w, heads, col0):
    b, s, _ = proj.shape
    slopes = jnp.exp2(-8.0 * jnp.arange(1, heads + 1, dtype=F32) / heads)
    slopes = jnp.broadcast_to(slopes[:, None, None], (heads, 1, HEAD))

    def col(group):
        return pl.BlockSpec((1, s, HEAD), lambda i, j, group=group: (i, 0, col0 + group * heads + j))

    npat = len(DILATIONS)
    return pl.pallas_call(
        functools.partial(_attn_kernel, seq=s),
        grid=(b, heads),
        in_specs=[col(0), col(1), col(2),
                  pl.BlockSpec((1, HEAD), lambda i, j: (0, 0)),
                  pl.BlockSpec((1, HEAD), lambda i, j: (0, 0)),
                  pl.BlockSpec((1, 1, HEAD), lambda i, j: (j, 0, 0))],
        out_specs=pl.BlockSpec((1, s, HEAD), lambda i, j: (i, 0, j)),
        out_shape=jax.ShapeDtypeStruct((b, s, heads * HEAD), BF16),
        scratch_shapes=[pltpu.VMEM((s, HEAD), F32), pltpu.VMEM((s, HEAD), F32),
                        pltpu.VMEM((npat, s, HEAD), F32), pltpu.VMEM((npat, s, HEAD), F32),
                        pltpu.VMEM((npat, s, HEAD), F32)],
        compiler_params=pltpu.CompilerParams(vmem_limit_bytes=VMEM_LIMIT),
        name="dilated_attn",
    )(proj, proj, proj, q_norm_w, k_norm_w, slopes)


def _outproj_kernel(a_ref, b_ref, wa_ref, wb_ref, x_ref, gate_ref, o_ref, wa_s, wb_s):
    @pl.when(pl.program_id(1) == 0)
    def _():
        wa_s[...] = wa_ref[...].astype(BF16)
        wb_s[...] = wb_ref[...].astype(BF16)

    mix = _dot(a_ref[...], wa_s[...]) + _dot(b_ref[...], wb_s[...])
    o_ref[...] = x_ref[...] + gate_ref[0] * mix


def _out_projection(a, b, w, x, gate, seq, tm=1024, tn=1024):
    m, ka = a.shape
    kb = b.shape[1]
    n = w.shape[1]
    per_seq = seq // tm
    return pl.pallas_call(
        _outproj_kernel,
        grid=(n // tn, m // tm),
        in_specs=[pl.BlockSpec((tm, ka), lambda j, i: (i, 0)),
                  pl.BlockSpec((tm, kb), lambda j, i: (i, 0)),
                  pl.BlockSpec((ka, tn), lambda j, i: (0, j)),
                  pl.BlockSpec((kb, tn), lambda j, i: (ka // kb, j)),
                  pl.BlockSpec((tm, tn), lambda j, i: (i, j)),
                  pl.BlockSpec((1, 1, tn), lambda j, i: (i // per_seq, 0, j))],
        out_specs=pl.BlockSpec((tm, tn), lambda j, i: (i, j)),
        out_shape=jax.ShapeDtypeStruct((m, n), F32),
        scratch_shapes=[pltpu.VMEM((ka, tn), BF16), pltpu.VMEM((kb, tn), BF16)],
        compiler_params=pltpu.CompilerParams(
            dimension_semantics=("arbitrary", "arbitrary"), vmem_limit_bytes=VMEM_LIMIT),
        name="out_proj",
    )(a, b, w, w, x, gate)


def _up_kernel(h_ref, wa_ref, wg_ref, cw_ref, cb_ref, y_ref, wa_s, wg_s, halo_s, *, per_seq):
    i = pl.program_id(1)

    @pl.when(i == 0)
    def _():
        wa_s[...] = wa_ref[...].astype(BF16)
        wg_s[...] = wg_ref[...].astype(BF16)

    @pl.when(i % per_seq == 0)
    def _():
        halo_s[...] = jnp.zeros_like(halo_s)

    h = h_ref[...]
    a = _dot(h, wa_s[...])
    g = _dot(h, wg_s[...])
    tm = a.shape[0]
    prev = halo_s[...]
    p1 = prev[7:8, :]
    p2 = prev[6:7, :]
    row = lax.broadcasted_iota(jnp.int32, a.shape, 0)
    a1 = jnp.where(row == 0, p1, pltpu.roll(a, 1, axis=0))
    a2 = jnp.where(row == 0, p2, jnp.where(row == 1, p1, pltpu.roll(a, 2, axis=0)))
    halo_s[...] = a[tm - 8:tm, :]
    cw = cw_ref[...]
    y = cb_ref[...] + a2 * cw[0:1, :] + a1 * cw[1:2, :] + a * cw[2:3, :]
    y_ref[...] = (_silu(y) * g).astype(y_ref.dtype)


def _up_projection(h, w, conv_w, conv_b, seq, tm=1024, tn=512):
    m, k = h.shape
    dff = w.shape[1] // 2
    nb = dff // tn
    return pl.pallas_call(
        functools.partial(_up_kernel, per_seq=seq // tm),
        grid=(nb, m // tm),
        in_specs=[pl.BlockSpec((tm, k), lambda j, i: (i, 0)),
                  pl.BlockSpec((k, tn), lambda j, i: (0, j)),
                  pl.BlockSpec((k, tn), lambda j, i: (0, nb + j)),
                  pl.BlockSpec((CONV_WIDTH, tn), lambda j, i: (0, j)),
                  pl.BlockSpec((1, tn), lambda j, i: (0, j))],
        out_specs=pl.BlockSpec((tm, tn), lambda j, i: (i, j)),
        out_shape=jax.ShapeDtypeStruct((m, dff), BF16),
        scratch_shapes=[pltpu.VMEM((k, tn), BF16), pltpu.VMEM((k, tn), BF16), pltpu.VMEM((8, tn), F32)],
        compiler_params=pltpu.CompilerParams(
            dimension_semantics=("arbitrary", "arbitrary"), vmem_limit_bytes=VMEM_LIMIT),
        name="up_proj_conv_gate",
    )(h, w, w, conv_w, conv_b)


def _down_kernel(y_ref, w_ref, x_ref, gate_ref, o_ref, w_s):
    @pl.when(pl.program_id(1) == 0)
    def _():
        w_s[...] = w_ref[...].astype(BF16)

    o_ref[...] = x_ref[...] + gate_ref[0] * _dot(y_ref[...], w_s[...])


def _down_projection(y, w, x, gate, seq, tm=512, tn=512):
    m, k = y.shape
    n = w.shape[1]
    per_seq = seq // tm
    return pl.pallas_call(
        _down_kernel,
        grid=(n // tn, m // tm),
        in_specs=[pl.BlockSpec((tm, k), lambda j, i: (i, 0)),
                  pl.BlockSpec((k, tn), lambda j, i: (0, j)),
                  pl.BlockSpec((tm, tn), lambda j, i: (i, j)),
                  pl.BlockSpec((1, 1, tn), lambda j, i: (i // per_seq, 0, j))],
        out_specs=pl.BlockSpec((tm, tn), lambda j, i: (i, j)),
        out_shape=jax.ShapeDtypeStruct((m, n), F32),
        scratch_shapes=[pltpu.VMEM((k, tn), BF16)],
        compiler_params=pltpu.CompilerParams(
            dimension_semantics=("arbitrary", "arbitrary"), vmem_limit_bytes=VMEM_LIMIT),
        name="down_proj",
    )(y, w, x, gate)


def _layer(layer, x, mod, norm1_w, w_in, lb_logits, hg_norm_w, q_norm_w, k_norm_w, w_out, norm2_w, w_up, conv_w,
           conv_b, w_down):
    b, s, d = x.shape
    shift1, scale1, gate1, shift2, scale2, gate2 = [mod[:b, j * d:(j + 1) * d].reshape(b, 1, d) for j in range(6)]

    h = _norm_modulate(x, norm1_w.reshape(1, d), scale1, shift1)
    proj = _in_projection(h.reshape(b * s, d), w_in).reshape(b, s, -1)
    a_out = _hgrn(proj, lb_logits, hg_norm_w.reshape(1, HEAD), HG_HEADS, layer)
    b_out = _attention(proj, q_norm_w.reshape(1, HEAD), k_norm_w.reshape(1, HEAD), ATT_HEADS, 4 * HG_HEADS)
    x1 = _out_projection(a_out.reshape(b * s, -1), b_out.reshape(b * s, -1), w_out, x.reshape(b * s, d), gate1, s)

    h2 = _norm_modulate(x1.reshape(b, s, d), norm2_w.reshape(1, d), scale2, shift2)
    y = _up_projection(h2.reshape(b * s, d), w_up, conv_w, conv_b.reshape(1, -1), s)
    x2 = _down_projection(y, w_down, x1, gate2, s)
    return x2.reshape(b, s, d)


def kernel(x, c, w_ada, b_ada, norm1_w, w_in, lb_logits, hg_norm_w, q_norm_w, k_norm_w, w_out, norm2_w, w_up, conv_w, conv_b, w_down):
    depth = w_ada.shape[0]
    b = x.shape[0]
    c_pad = jnp.pad(c, ((0, 8 - b), (0, 0)))
    for l in range(depth):
        mod = _modulation(c_pad, w_ada[l], b_ada[l].reshape(1, -1))
        x = _layer(l, x, mod, norm1_w[l], w_in[l], lb_logits, hg_norm_w[l], q_norm_w[l], k_norm_w[l],
                   w_out[l], norm2_w[l], w_up[l], conv_w[l], conv_b[l], w_down[l])
    return x
```

```python
import functools

import numpy as np
import jax
import jax.numpy as jnp
from jax import lax
from jax.experimental import pallas as pl
from jax.experimental.pallas import tpu as pltpu

F32 = jnp.float32
BF16 = jnp.bfloat16
EPS = 1e-6
LOG2E = 1.4426950408889634

HEAD = 128
HG_HEADS = 8
ATT_HEADS = 8
N_BACK = 128
DILATIONS = (1, 4, 16)
CONV_WIDTH = 3
HG_CHUNK = 64
VMEM_LIMIT = 56 * 1024 * 1024


def _dot(a, b):
    return jnp.dot(a, b, preferred_element_type=F32)


def _dot_nt(a, b):
    return lax.dot_general(a, b, (((1,), (1,)), ((), ())), preferred_element_type=F32)


def _dot_tn(a, b):
    return lax.dot_general(a, b, (((0,), (0,)), ((), ())), preferred_element_type=F32)


def _silu(v):
    return v * jax.nn.sigmoid(v)


def _mod_kernel(c_ref, w_ref, b_ref, o_ref):
    ca = _silu(c_ref[...]).astype(BF16)
    o_ref[...] = _dot(ca, w_ref[...].astype(BF16)) + b_ref[...]


def _modulation(c_pad, w_ada, b_ada, tn=1024):
    rows, d = c_pad.shape
    n = w_ada.shape[1]
    return pl.pallas_call(
        _mod_kernel,
        grid=(n // tn,),
        in_specs=[pl.BlockSpec((rows, d), lambda j: (0, 0)),
                  pl.BlockSpec((d, tn), lambda j: (0, j)),
                  pl.BlockSpec((1, tn), lambda j: (0, j))],
        out_specs=pl.BlockSpec((rows, tn), lambda j: (0, j)),
        out_shape=jax.ShapeDtypeStruct((rows, n), F32),
        compiler_params=pltpu.CompilerParams(vmem_limit_bytes=VMEM_LIMIT),
        name="adaln_mod",
    )(c_pad, w_ada, b_ada)


def _norm_kernel(x_ref, w_ref, scale_ref, shift_ref, o_ref):
    x = x_ref[0]
    ms = jnp.mean(x * x, axis=-1, keepdims=True)
    y = x * lax.rsqrt(ms + EPS) * w_ref[...]
    o_ref[0] = (y * (1.0 + scale_ref[0]) + shift_ref[0]).astype(o_ref.dtype)


def _norm_modulate(x, w, scale, shift, ts=512):
    b, s, d = x.shape
    return pl.pallas_call(
        _norm_kernel,
        grid=(b, s // ts),
        in_specs=[pl.BlockSpec((1, ts, d), lambda i, j: (i, j, 0)),
                  pl.BlockSpec((1, d), lambda i, j: (0, 0)),
                  pl.BlockSpec((1, 1, d), lambda i, j: (i, 0, 0)),
                  pl.BlockSpec((1, 1, d), lambda i, j: (i, 0, 0))],
        out_specs=pl.BlockSpec((1, ts, d), lambda i, j: (i, j, 0)),
        out_shape=jax.ShapeDtypeStruct((b, s, d), BF16),
        compiler_params=pltpu.CompilerParams(vmem_limit_bytes=VMEM_LIMIT),
        name="norm_modulate",
    )(x, w, scale, shift)


def _inproj_kernel(x_ref, w_ref, nw_ref, o_ref, wb_ref, *, norm_tiles):
    j = pl.program_id(0)

    @pl.when(pl.program_id(1) == 0)
    def _():
        wb_ref[...] = w_ref[...].astype(BF16)

    o_ref[...] = _dot(x_ref[...], wb_ref[...])

    @pl.when(functools.reduce(jnp.logical_or, [j == t for t in norm_tiles]))
    def _():
        nw = nw_ref[0]
        for c0 in range(0, o_ref.shape[1], HEAD):
            seg = o_ref[:, c0:c0 + HEAD]
            ms = jnp.mean(seg * seg, axis=-1, keepdims=True)
            o_ref[:, c0:c0 + HEAD] = seg * lax.rsqrt(ms + EPS) * nw[:, c0:c0 + HEAD]


def _in_projection(h, w, norm_w, norm_tiles, tm=1024, tn=1024):
    m, k = h.shape
    n = w.shape[1]
    return pl.pallas_call(
        functools.partial(_inproj_kernel, norm_tiles=norm_tiles),
        grid=(n // tn, m // tm),
        in_specs=[pl.BlockSpec((tm, k), lambda j, i: (i, 0)),
                  pl.BlockSpec((k, tn), lambda j, i: (0, j)),
                  pl.BlockSpec((1, 1, tn), lambda j, i: (j, 0, 0))],
        out_specs=pl.BlockSpec((tm, tn), lambda j, i: (i, j)),
        out_shape=jax.ShapeDtypeStruct((m, n), F32),
        scratch_shapes=[pltpu.VMEM((k, tn), BF16)],
        compiler_params=pltpu.CompilerParams(
            dimension_semantics=("arbitrary", "arbitrary"), vmem_limit_bytes=VMEM_LIMIT),
        name="in_proj",
    )(h, w, norm_w)


def _hgrn_tables(c):
    t = np.arange(c)[:, None]
    u = np.arange(c)[None, :]
    masks = []
    m = c
    while m >= 2:
        half = m // 2
        start = (t // m) * m
        u_start = (u // m) * m
        masks.append(((t - start) >= half) & ((u - u_start) < half) & (u_start == start))
        m //= 2
    cum = (u <= t).astype(np.float32)
    return np.concatenate([cum, cum, cum], axis=1), np.stack(masks).astype(np.float32)


def _level_signs(c, width):
    t = lax.broadcasted_iota(jnp.int32, (c, width), 0)
    signs = []
    m = c
    while m >= 8:
        signs.append(jnp.where(t % m >= m // 2, 1.0, -1.0).astype(F32))
        m //= 2
    return signs


def _level_log_decays(g, cum, signs):
    c = g.shape[0]
    out = []
    m = c
    for sgn in signs:
        half = m // 2
        mids = [jnp.broadcast_to(cum[j * m + half - 1:j * m + half, :], (m, g.shape[1])) for j in range(c // m)]
        out.append((cum - jnp.concatenate(mids, axis=0)) * sgn)
        m //= 2
    pos = lax.broadcasted_iota(jnp.int32, g.shape, 0) % 4
    g_next = pltpu.roll(g, c - 1, axis=0)
    g_prev = pltpu.roll(g, 1, axis=0)
    out.append(jnp.where(pos == 0, g_next, jnp.where(pos == 1, 0.0, jnp.where(pos == 2, g, g + g_prev))))
    out.append(jnp.where(pos % 2 == 1, g, 0.0))
    return out


def _hgrn_kernel(hq_ref, hf_ref, hi_ref, hg_ref, lbl_ref, nw_ref, tab_ref, msk_ref, o_ref, st_ref, *,
                 chunk, heads, layer):
    c = chunk
    assert chunk >= 8 and 2 ** msk_ref.shape[0] == chunk
    ts = hq_ref.shape[1]

    @pl.when(pl.program_id(1) == 0)
    def _():
        st_ref[...] = jnp.zeros_like(st_ref)

    lbl = lbl_ref[...]
    ex = jnp.exp(lbl - jnp.max(lbl, axis=0, keepdims=True))
    lb_all = jnp.sum(ex[0:layer + 1], axis=0, keepdims=True) / jnp.sum(ex, axis=0, keepdims=True)
    nw = nw_ref[...]
    eye = (lax.broadcasted_iota(jnp.int32, (c, c), 0) == lax.broadcasted_iota(jnp.int32, (c, c), 1))
    signs = _level_signs(c, HEAD)

    def one_head(rows, h):
        cols = slice(h * HEAD, (h + 1) * HEAD)
        lb = lb_all[:, cols]
        q = _silu(hq_ref[0, rows, cols])
        f = lb + (1.0 - lb) * jax.nn.sigmoid(hf_ref[0, rows, cols])
        g = jnp.log(f) * LOG2E
        kk = 1.0 - f
        v = hi_ref[0, rows, cols].astype(BF16)

        g1 = g.astype(BF16)
        r1 = g - g1.astype(F32)
        g2 = r1.astype(BF16)
        g3 = (r1 - g2.astype(F32)).astype(BF16)
        cum = _dot(tab_ref[...], jnp.concatenate([g1, g2, g3], axis=0))
        e_cum = jnp.exp2(cum)
        e_last = jnp.exp2(cum[c - 1:c, :] - cum)

        state_t = st_ref[h]
        o = _dot_nt((q * e_cum).astype(BF16), state_t.astype(BF16))

        a = jnp.where(eye, _dot_nt(q.astype(BF16), kk.astype(BF16)), 0.0)
        for l, d in enumerate(_level_log_decays(g, cum, signs)):
            e = jnp.exp2(d)
            sc = _dot_nt((q * e).astype(BF16), (kk * e).astype(BF16))
            a = jnp.where(msk_ref[l] > 0.5, sc, a)
        o = o + _dot(a.astype(BF16), v)

        k_dec = (kk * e_last).astype(BF16)
        st_ref[h] = state_t * e_cum[c - 1:c, :] + _dot_tn(v, k_dec)

        ms = jnp.mean(o * o, axis=-1, keepdims=True)
        on = o * lax.rsqrt(ms + EPS) * nw
        o_ref[0, rows, cols] = (on * _silu(hg_ref[0, rows, cols])).astype(o_ref.dtype)

    def body(i, carry):
        rows = pl.ds(pl.multiple_of(i * c, c), c)
        for h in range(heads):
            one_head(rows, h)
        return carry

    lax.fori_loop(0, ts // c, body, 0, unroll=2)


def _hgrn(proj, lb_logits, norm_w, heads, layer, ts=512):
    b, s, _ = proj.shape
    nl = lb_logits.shape[0]
    w = heads * HEAD
    tab3, msk = _hgrn_tables(HG_CHUNK)
    tab3 = jnp.asarray(tab3, BF16)
    msk = jnp.asarray(msk, F32)

    def col(group):
        return pl.BlockSpec((1, ts, w), lambda i, j, group=group: (i, j, group))

    return pl.pallas_call(
        functools.partial(_hgrn_kernel, chunk=HG_CHUNK, heads=heads, layer=layer),
        grid=(b, s // ts),
        in_specs=[col(0), col(1), col(2), col(3),
                  pl.BlockSpec((nl, w), lambda i, j: (0, 0)),
                  pl.BlockSpec((1, HEAD), lambda i, j: (0, 0)),
                  pl.BlockSpec(tab3.shape, lambda i, j: (0, 0)),
                  pl.BlockSpec(msk.shape, lambda i, j: (0, 0, 0))],
        out_specs=pl.BlockSpec((1, ts, w), lambda i, j: (i, j, 0)),
        out_shape=jax.ShapeDtypeStruct((b, s, w), BF16),
        scratch_shapes=[pltpu.VMEM((heads, HEAD, HEAD), F32)],
        compiler_params=pltpu.CompilerParams(
            dimension_semantics=("arbitrary", "arbitrary"), vmem_limit_bytes=VMEM_LIMIT),
        name="hgrn2",
    )(proj, proj, proj, proj, lb_logits, norm_w, tab3, msk)


def _attn_kernel(q_ref, k_ref, v_ref, slope_ref, o_ref, o_s, m_s, d_s, *, seq):
    assert DILATIONS == (1, 4, 16) and seq == 16 * N_BACK
    blk = N_BACK
    npat = len(DILATIONS)
    slope = slope_ref[0]
    rows_n = 256

    qi = lax.broadcasted_iota(jnp.int32, (blk, 2 * blk), 0)
    kj = lax.broadcasted_iota(jnp.int32, (blk, 2 * blk), 1)
    steps = qi + blk - kj
    ok = (steps >= 0) & (steps <= N_BACK)
    stepsf = steps.astype(F32)

    def one_block(p, dil, base, bias, with_prev):
        qb = q_ref[0, pl.ds(base, blk, stride=dil), :].astype(BF16)
        if with_prev:
            kb = k_ref[0, pl.ds(base - blk * dil, 2 * blk, stride=dil), :].astype(BF16)
            vb = v_ref[0, pl.ds(base - blk * dil, 2 * blk, stride=dil), :].astype(BF16)
            sc = _dot_nt(qb, kb) + bias
        else:
            kb = k_ref[0, pl.ds(base, blk, stride=dil), :].astype(BF16)
            vb = v_ref[0, pl.ds(base, blk, stride=dil), :].astype(BF16)
            sc = _dot_nt(qb, kb) + bias[:, blk:]
        m = jnp.max(sc, axis=-1, keepdims=True)
        pr = jnp.exp(sc - m).astype(BF16)
        res = _dot(pr, jnp.concatenate([vb, jnp.ones_like(vb)], axis=1))
        o_s[p, pl.ds(base, blk, stride=dil), :] = res[:, :HEAD]
        d_s[p, pl.ds(base, blk, stride=dil), :] = res[:, HEAD:]
        m_s[p, pl.ds(base, blk, stride=dil), :] = jnp.broadcast_to(m, (blk, HEAD))

    def bias_for(dil):
        return jnp.where(ok, -slope[:, 0:1] * (stepsf * float(dil)), -jnp.inf)

    bias1, bias4, bias16 = [bias_for(d) for d in DILATIONS]
    one_block(0, 1, 0, bias1, False)
    for r in range(4):
        one_block(1, 4, r, bias4, False)

    for n in range(1, seq // blk):
        one_block(0, 1, n * blk, bias1, True)
    for n in range(1, seq // (4 * blk)):
        for r in range(4):
            one_block(1, 4, r + n * (4 * blk), bias4, True)
    for r in range(16):
        one_block(2, 16, r, bias16, False)

    def merge_body(i, carry):
        rows = pl.ds(pl.multiple_of(i * rows_n, rows_n), rows_n)
        ms = [m_s[p, rows, :] for p in range(npat)]
        m = functools.reduce(jnp.maximum, ms)
        es = [jnp.exp(mp - m) for mp in ms]
        den = functools.reduce(lambda a, b: a + b, [e * d_s[p, rows, :] for p, e in enumerate(es)])
        num = functools.reduce(lambda a, b: a + b, [e * o_s[p, rows, :] for p, e in enumerate(es)])
        o_ref[0, rows, :] = (num / den).astype(o_ref.dtype)
        return carry

    lax.fori_loop(0, seq // rows_n, merge_body, 0)


def _attention(proj, heads, col0):
    b, s, _ = proj.shape
    slopes = jnp.exp2(-8.0 * jnp.arange(1, heads + 1, dtype=F32) / heads)
    slopes = jnp.broadcast_to(slopes[:, None, None], (heads, 1, HEAD))

    def col(group):
        return pl.BlockSpec((1, s, HEAD), lambda i, j, group=group: (i, 0, col0 + group * heads + j))

    npat = len(DILATIONS)
    return pl.pallas_call(
        functools.partial(_attn_kernel, seq=s),
        grid=(b, heads),
        in_specs=[col(0), col(1), col(2),
                  pl.BlockSpec((1, 1, HEAD), lambda i, j: (j, 0, 0))],
        out_specs=pl.BlockSpec((1, s, HEAD), lambda i, j: (i, 0, j)),
        out_shape=jax.ShapeDtypeStruct((b, s, heads * HEAD), BF16),
        scratch_shapes=[pltpu.VMEM((npat, s, HEAD), F32), pltpu.VMEM((npat, s, HEAD), F32),
                        pltpu.VMEM((npat, s, HEAD), F32)],
        compiler_params=pltpu.CompilerParams(vmem_limit_bytes=VMEM_LIMIT),
        name="dilated_attn",
    )(proj, proj, proj, slopes)


def _outproj_norm_kernel(a_ref, b_ref, w_ref, x_ref, gate_ref, nw_ref, scale_ref, shift_ref, x1_ref, h_ref):
    ka = a_ref.shape[1]
    mix = _dot(a_ref[...], w_ref[0:ka, :]) + _dot(b_ref[...], w_ref[ka:, :])
    x1 = x_ref[...] + gate_ref[0] * mix
    x1_ref[...] = x1
    ms = jnp.mean(x1 * x1, axis=-1, keepdims=True)
    y = x1 * lax.rsqrt(ms + EPS) * nw_ref[...]
    h_ref[...] = (y * (1.0 + scale_ref[0]) + shift_ref[0]).astype(h_ref.dtype)


def _out_projection_norm(a, b, w, x, gate, norm_w, scale, shift, seq, tm=512):
    m, ka = a.shape
    kb = b.shape[1]
    n = w.shape[1]
    per_seq = seq // tm

    def per_batch():
        return pl.BlockSpec((1, 1, n), lambda i: (i // per_seq, 0, 0))

    return pl.pallas_call(
        _outproj_norm_kernel,
        grid=(m // tm,),
        in_specs=[pl.BlockSpec((tm, ka), lambda i: (i, 0)),
                  pl.BlockSpec((tm, kb), lambda i: (i, 0)),
                  pl.BlockSpec((ka + kb, n), lambda i: (0, 0)),
                  pl.BlockSpec((tm, n), lambda i: (i, 0)),
                  per_batch(),
                  pl.BlockSpec((1, n), lambda i: (0, 0)),
                  per_batch(), per_batch()],
        out_specs=[pl.BlockSpec((tm, n), lambda i: (i, 0)),
                   pl.BlockSpec((tm, n), lambda i: (i, 0))],
        out_shape=[jax.ShapeDtypeStruct((m, n), F32), jax.ShapeDtypeStruct((m, n), BF16)],
        compiler_params=pltpu.CompilerParams(vmem_limit_bytes=VMEM_LIMIT),
        name="out_proj_norm",
    )(a, b, w, x, gate, norm_w, scale, shift)


def _up_kernel(h_ref, wa_ref, wg_ref, cw_ref, cb_ref, y_ref, wa_s, wg_s, halo_s, *, per_seq):
    i = pl.program_id(1)

    @pl.when(i == 0)
    def _():
        wa_s[...] = wa_ref[...].astype(BF16)
        wg_s[...] = wg_ref[...].astype(BF16)

    @pl.when(i % per_seq == 0)
    def _():
        halo_s[...] = jnp.zeros_like(halo_s)

    h = h_ref[...]
    a = _dot(h, wa_s[...])
    g = _dot(h, wg_s[...])
    tm = a.shape[0]
    prev = halo_s[...]
    p1 = prev[7:8, :]
    p2 = prev[6:7, :]
    row = lax.broadcasted_iota(jnp.int32, a.shape, 0)
    a1 = jnp.where(row == 0, p1, pltpu.roll(a, 1, axis=0))
    a2 = jnp.where(row == 0, p2, jnp.where(row == 1, p1, pltpu.roll(a, 2, axis=0)))
    halo_s[...] = a[tm - 8:tm, :]
    cw = cw_ref[...]
    y = cb_ref[...] + a2 * cw[0:1, :] + a1 * cw[1:2, :] + a * cw[2:3, :]
    y_ref[...] = (_silu(y) * g).astype(y_ref.dtype)


def _up_projection(h, w, conv_w, conv_b, seq, tm=1024, tn=512):
    m, k = h.shape
    dff = w.shape[1] // 2
    nb = dff // tn
    return pl.pallas_call(
        functools.partial(_up_kernel, per_seq=seq // tm),
        grid=(nb, m // tm),
        in_specs=[pl.BlockSpec((tm, k), lambda j, i: (i, 0)),
                  pl.BlockSpec((k, tn), lambda j, i: (0, j)),
                  pl.BlockSpec((k, tn), lambda j, i: (0, nb + j)),
                  pl.BlockSpec((CONV_WIDTH, tn), lambda j, i: (0, j)),
                  pl.BlockSpec((1, tn), lambda j, i: (0, j))],
        out_specs=pl.BlockSpec((tm, tn), lambda j, i: (i, j)),
        out_shape=jax.ShapeDtypeStruct((m, dff), BF16),
        scratch_shapes=[pltpu.VMEM((k, tn), BF16), pltpu.VMEM((k, tn), BF16), pltpu.VMEM((8, tn), F32)],
        compiler_params=pltpu.CompilerParams(
            dimension_semantics=("arbitrary", "arbitrary"), vmem_limit_bytes=VMEM_LIMIT),
        name="up_proj_conv_gate",
    )(h, w, w, conv_w, conv_b)


def _down_kernel(y_ref, w_ref, x_ref, gate_ref, o_ref, w_s):
    @pl.when(pl.program_id(1) == 0)
    def _():
        w_s[...] = w_ref[...].astype(BF16)

    o_ref[...] = x_ref[...] + gate_ref[0] * _dot(y_ref[...], w_s[...])


def _down_projection(y, w, x, gate, seq, tm=512, tn=512):
    m, k = y.shape
    n = w.shape[1]
    per_seq = seq // tm
    return pl.pallas_call(
        _down_kernel,
        grid=(n // tn, m // tm),
        in_specs=[pl.BlockSpec((tm, k), lambda j, i: (i, 0)),
                  pl.BlockSpec((k, tn), lambda j, i: (0, j)),
                  pl.BlockSpec((tm, tn), lambda j, i: (i, j)),
                  pl.BlockSpec((1, 1, tn), lambda j, i: (i // per_seq, 0, j))],
        out_specs=pl.BlockSpec((tm, tn), lambda j, i: (i, j)),
        out_shape=jax.ShapeDtypeStruct((m, n), F32),
        scratch_shapes=[pltpu.VMEM((k, tn), BF16)],
        compiler_params=pltpu.CompilerParams(
            dimension_semantics=("arbitrary", "arbitrary"), vmem_limit_bytes=VMEM_LIMIT),
        name="down_proj",
    )(y, w, x, gate)


def _layer(layer, x, mod, norm1_w, w_in, lb_logits, hg_norm_w, q_norm_w, k_norm_w, w_out, norm2_w, w_up, conv_w,
           conv_b, w_down):
    b, s, d = x.shape
    shift1, scale1, gate1, shift2, scale2, gate2 = [mod[:b, j * d:(j + 1) * d].reshape(b, 1, d) for j in range(6)]

    h = _norm_modulate(x, norm1_w.reshape(1, d), scale1, shift1)
    tn = ATT_HEADS * HEAD
    q_tile = 4 * HG_HEADS * HEAD // tn
    n_tiles = w_in.shape[1] // tn
    head_w = jnp.ones((n_tiles, 1, tn), F32)
    head_w = head_w.at[q_tile, 0].set(jnp.tile(q_norm_w * (HEAD ** -0.5), ATT_HEADS))
    head_w = head_w.at[q_tile + 1, 0].set(jnp.tile(k_norm_w, ATT_HEADS))
    proj = _in_projection(h.reshape(b * s, d), w_in, head_w, (q_tile, q_tile + 1), tn=tn).reshape(b, s, -1)
    a_out = _hgrn(proj, lb_logits, hg_norm_w.reshape(1, HEAD), HG_HEADS, layer)
    b_out = _attention(proj, ATT_HEADS, 4 * HG_HEADS)
    x1, h2 = _out_projection_norm(a_out.reshape(b * s, -1), b_out.reshape(b * s, -1), w_out.astype(BF16),
                                  x.reshape(b * s, d), gate1, norm2_w.reshape(1, d), scale2, shift2, s)
    y = _up_projection(h2, w_up, conv_w, conv_b.reshape(1, -1), s)
    x2 = _down_projection(y, w_down, x1, gate2, s)
    return x2.reshape(b, s, d)


def kernel(x, c, w_ada, b_ada, norm1_w, w_in, lb_logits, hg_norm_w, q_norm_w, k_norm_w, w_out, norm2_w, w_up, conv_w, conv_b, w_down):
    depth = w_ada.shape[0]
    b = x.shape[0]
    c_pad = jnp.pad(c, ((0, 8 - b), (0, 0)))
    for l in range(depth):
        mod = _modulation(c_pad, w_ada[l], b_ada[l].reshape(1, -1))
        x = _layer(l, x, mod, norm1_w[l], w_in[l], lb_logits, hg_norm_w[l], q_norm_w[l], k_norm_w[l],
                   w_out[l], norm2_w[l], w_up[l], conv_w[l], conv_b[l], w_down[l])
    return x
```

```python
import functools

import numpy as np
import jax
import jax.numpy as jnp
from jax import lax
from jax.experimental import pallas as pl
from jax.experimental.pallas import tpu as pltpu

F32 = jnp.float32
BF16 = jnp.bfloat16
EPS = 1e-6
LOG2E = 1.4426950408889634

HEAD = 128
HG_HEADS = 8
ATT_HEADS = 8
N_BACK = 128
DILATIONS = (1, 4, 16)
CONV_WIDTH = 3
HG_CHUNK = 64
HG_UNROLL = 4
HG_LOOKAHEAD = 2
ATT_LOOKAHEAD = 2
VMEM_LIMIT = 56 * 1024 * 1024


def _dot(a, b):
    return jnp.dot(a, b, preferred_element_type=F32)


def _dot_nt(a, b):
    return lax.dot_general(a, b, (((1,), (1,)), ((), ())), preferred_element_type=F32)


def _dot_tn(a, b):
    return lax.dot_general(a, b, (((0,), (0,)), ((), ())), preferred_element_type=F32)


def _silu(v):
    return v * jax.nn.sigmoid(v)


def _mod_kernel(c_ref, w_ref, b_ref, o_ref):
    ca = _silu(c_ref[...]).astype(BF16)
    o_ref[...] = _dot(ca, w_ref[...].astype(BF16)) + b_ref[...]


def _modulation(c_pad, w_ada, b_ada, tn=1024):
    rows, d = c_pad.shape
    n = w_ada.shape[1]
    return pl.pallas_call(
        _mod_kernel,
        grid=(n // tn,),
        in_specs=[pl.BlockSpec((rows, d), lambda j: (0, 0)),
                  pl.BlockSpec((d, tn), lambda j: (0, j)),
                  pl.BlockSpec((1, tn), lambda j: (0, j))],
        out_specs=pl.BlockSpec((rows, tn), lambda j: (0, j)),
        out_shape=jax.ShapeDtypeStruct((rows, n), F32),
        compiler_params=pltpu.CompilerParams(vmem_limit_bytes=VMEM_LIMIT),
        name="adaln_mod",
    )(c_pad, w_ada, b_ada)


def _norm_kernel(x_ref, w_ref, scale_ref, shift_ref, o_ref):
    x = x_ref[0]
    ms = jnp.mean(x * x, axis=-1, keepdims=True)
    y = x * lax.rsqrt(ms + EPS) * w_ref[...]
    o_ref[0] = (y * (1.0 + scale_ref[0]) + shift_ref[0]).astype(o_ref.dtype)


def _norm_modulate(x, w, scale, shift, ts=512):
    b, s, d = x.shape
    return pl.pallas_call(
        _norm_kernel,
        grid=(b, s // ts),
        in_specs=[pl.BlockSpec((1, ts, d), lambda i, j: (i, j, 0)),
                  pl.BlockSpec((1, d), lambda i, j: (0, 0)),
                  pl.BlockSpec((1, 1, d), lambda i, j: (i, 0, 0)),
                  pl.BlockSpec((1, 1, d), lambda i, j: (i, 0, 0))],
        out_specs=pl.BlockSpec((1, ts, d), lambda i, j: (i, j, 0)),
        out_shape=jax.ShapeDtypeStruct((b, s, d), BF16),
        compiler_params=pltpu.CompilerParams(vmem_limit_bytes=VMEM_LIMIT),
        name="norm_modulate",
    )(x, w, scale, shift)


def _inproj_kernel(x_ref, w_ref, nw_ref, o_ref, wb_ref, *, norm_tiles):
    j = pl.program_id(0)

    @pl.when(pl.program_id(1) == 0)
    def _():
        wb_ref[...] = w_ref[...].astype(BF16)

    o_ref[...] = _dot(x_ref[...], wb_ref[...])

    @pl.when(functools.reduce(jnp.logical_or, [j == t for t in norm_tiles]))
    def _():
        nw = nw_ref[0]
        for c0 in range(0, o_ref.shape[1], HEAD):
            seg = o_ref[:, c0:c0 + HEAD]
            ms = jnp.mean(seg * seg, axis=-1, keepdims=True)
            o_ref[:, c0:c0 + HEAD] = seg * lax.rsqrt(ms + EPS) * nw[:, c0:c0 + HEAD]


def _in_projection(h, w, norm_w, norm_tiles, tm=1024, tn=1024):
    m, k = h.shape
    n = w.shape[1]
    return pl.pallas_call(
        functools.partial(_inproj_kernel, norm_tiles=norm_tiles),
        grid=(n // tn, m // tm),
        in_specs=[pl.BlockSpec((tm, k), lambda j, i: (i, 0)),
                  pl.BlockSpec((k, tn), lambda j, i: (0, j)),
                  pl.BlockSpec((1, 1, tn), lambda j, i: (j, 0, 0))],
        out_specs=pl.BlockSpec((tm, tn), lambda j, i: (i, j)),
        out_shape=jax.ShapeDtypeStruct((m, n), F32),
        scratch_shapes=[pltpu.VMEM((k, tn), BF16)],
        compiler_params=pltpu.CompilerParams(
            dimension_semantics=("arbitrary", "arbitrary"), vmem_limit_bytes=VMEM_LIMIT),
        name="in_proj",
    )(h, w, norm_w)


def _hgrn_tables(c):
    t = np.arange(c)[:, None]
    u = np.arange(c)[None, :]
    masks = []
    m = c
    while m >= 2:
        half = m // 2
        start = (t // m) * m
        u_start = (u // m) * m
        masks.append(((t - start) >= half) & ((u - u_start) < half) & (u_start == start))
        m //= 2
    cum = (u <= t).astype(np.float32)
    return np.concatenate([cum, cum, cum], axis=1), np.stack(masks).astype(np.float32)


def _level_signs(c, width):
    t = lax.broadcasted_iota(jnp.int32, (c, width), 0)
    signs = []
    m = c
    while m >= 8:
        signs.append(jnp.where(t % m >= m // 2, 1.0, -1.0).astype(F32))
        m //= 2
    return signs


def _level_log_decays(g, cum, signs):
    c = g.shape[0]
    out = []
    m = c
    for sgn in signs:
        half = m // 2
        mids = [jnp.broadcast_to(cum[j * m + half - 1:j * m + half, :], (m, g.shape[1])) for j in range(c // m)]
        out.append((cum - jnp.concatenate(mids, axis=0)) * sgn)
        m //= 2
    pos = lax.broadcasted_iota(jnp.int32, g.shape, 0) % 4
    g_next = pltpu.roll(g, c - 1, axis=0)
    g_prev = pltpu.roll(g, 1, axis=0)
    out.append(jnp.where(pos == 0, g_next, jnp.where(pos == 1, 0.0, jnp.where(pos == 2, g, g + g_prev))))
    return out


def _hgrn_kernel(hq_ref, hf_ref, hi_ref, hg_ref, lbl_ref, nw_ref, tab_ref, msk_ref, o_ref, st_ref, *,
                 chunk, heads, layer):
    c = chunk
    assert chunk >= 8 and 2 ** msk_ref.shape[0] == chunk
    ts = hq_ref.shape[1]

    @pl.when(pl.program_id(1) == 0)
    def _():
        st_ref[...] = jnp.zeros_like(st_ref)

    lbl = lbl_ref[...]
    ex = jnp.exp(lbl - jnp.max(lbl, axis=0, keepdims=True))
    lb_all = jnp.sum(ex[0:layer + 1], axis=0, keepdims=True) / jnp.sum(ex, axis=0, keepdims=True)
    nw = nw_ref[...]
    t_idx = lax.broadcasted_iota(jnp.int32, (c, c), 0)
    s_idx = lax.broadcasted_iota(jnp.int32, (c, c), 1)
    near_diag = (s_idx == t_idx) | ((t_idx % 2 == 1) & (s_idx == t_idx - 1))
    odd = lax.broadcasted_iota(jnp.int32, (c, HEAD), 0) % 2 == 1
    signs = _level_signs(c, HEAD)

    def gates(rows, h):
        cols = slice(h * HEAD, (h + 1) * HEAD)
        lb = lb_all[:, cols]
        q = _silu(hq_ref[0, rows, cols])
        f = lb + (1.0 - lb) * jax.nn.sigmoid(hf_ref[0, rows, cols])
        g = jnp.log(f) * LOG2E
        g1 = g.astype(BF16)
        r1 = g - g1.astype(F32)
        g2 = r1.astype(BF16)
        g3 = (r1 - g2.astype(F32)).astype(BF16)
        cum = _dot(tab_ref[...], jnp.concatenate([g1, g2, g3], axis=0))
        return rows, h, q, f, g, cum

    def pair_products(rows, h, q, f, g, cum):
        cols = slice(h * HEAD, (h + 1) * HEAD)
        kk = 1.0 - f
        v = hi_ref[0, rows, cols].astype(BF16)
        e_cum = jnp.exp2(cum)
        e_last = jnp.exp2(cum[c - 1:c, :] - cum)

        state_t = st_ref[h]
        o = _dot_nt((q * e_cum).astype(BF16), state_t.astype(BF16))
        k_dec = (kk * e_last).astype(BF16)
        st_ref[h] = state_t * e_cum[c - 1:c, :] + _dot_tn(v, k_dec)

        e2 = jnp.where(odd, f, 1.0)
        lhs = jnp.concatenate([q * e2, jnp.where(odd, q * (1.0 - f * f), 0.0)], axis=1).astype(BF16)
        rhs = jnp.concatenate([kk * e2, jnp.where(odd, kk, 0.0)], axis=1).astype(BF16)
        a = jnp.where(near_diag, _dot_nt(lhs, rhs), 0.0)
        for l, d in enumerate(_level_log_decays(g, cum, signs)):
            e = jnp.exp2(d)
            sc = _dot_nt((q * e).astype(BF16), (kk * e).astype(BF16))
            a = jnp.where(msk_ref[l] > 0.5, sc, a)
        return rows, h, o, a, v

    def output(rows, h, o, a, v):
        cols = slice(h * HEAD, (h + 1) * HEAD)
        o = o + _dot(a.astype(BF16), v)
        ms = jnp.mean(o * o, axis=-1, keepdims=True)
        on = o * lax.rsqrt(ms + EPS) * nw
        o_ref[0, rows, cols] = (on * _silu(hg_ref[0, rows, cols])).astype(o_ref.dtype)

    def body(i, carry):
        after_gates, after_pairs = [], []
        for u in range(HG_UNROLL):
            rows = pl.ds(pl.multiple_of((i * HG_UNROLL + u) * c, c), c)
            for h in range(heads):
                after_gates.append(gates(rows, h))
                if len(after_gates) > HG_LOOKAHEAD:
                    after_pairs.append(pair_products(*after_gates.pop(0)))
                if len(after_pairs) > HG_LOOKAHEAD:
                    output(*after_pairs.pop(0))
        for item in after_gates:
            after_pairs.append(pair_products(*item))
        for item in after_pairs:
            output(*item)
        return carry

    lax.fori_loop(0, ts // (c * HG_UNROLL), body, 0)


def _hgrn(proj, lb_logits, norm_w, heads, layer, ts=512):
    b, s, _ = proj.shape
    nl = lb_logits.shape[0]
    w = heads * HEAD
    tab3, msk = _hgrn_tables(HG_CHUNK)
    tab3 = jnp.asarray(tab3, BF16)
    msk = jnp.asarray(msk, F32)

    def col(group):
        return pl.BlockSpec((1, ts, w), lambda i, j, group=group: (i, j, group))

    return pl.pallas_call(
        functools.partial(_hgrn_kernel, chunk=HG_CHUNK, heads=heads, layer=layer),
        grid=(b, s // ts),
        in_specs=[col(0), col(1), col(2), col(3),
                  pl.BlockSpec((nl, w), lambda i, j: (0, 0)),
                  pl.BlockSpec((1, HEAD), lambda i, j: (0, 0)),
                  pl.BlockSpec(tab3.shape, lambda i, j: (0, 0)),
                  pl.BlockSpec(msk.shape, lambda i, j: (0, 0, 0))],
        out_specs=pl.BlockSpec((1, ts, w), lambda i, j: (i, j, 0)),
        out_shape=jax.ShapeDtypeStruct((b, s, w), BF16),
        scratch_shapes=[pltpu.VMEM((heads, HEAD, HEAD), F32)],
        compiler_params=pltpu.CompilerParams(
            dimension_semantics=("arbitrary", "arbitrary"), vmem_limit_bytes=VMEM_LIMIT),
        name="hgrn2",
    )(proj, proj, proj, proj, lb_logits, norm_w, tab3, msk)


def _attn_kernel(q_ref, k_ref, v_ref, slope_ref, o_ref, o_s, m_s, d_s, *, seq):
    assert DILATIONS == (1, 4, 16) and seq == 16 * N_BACK
    blk = N_BACK
    npat = len(DILATIONS)
    slope = slope_ref[0]
    rows_n = 256

    qi = lax.broadcasted_iota(jnp.int32, (blk, 2 * blk), 0)
    kj = lax.broadcasted_iota(jnp.int32, (blk, 2 * blk), 1)
    steps = qi + blk - kj
    ok = (steps >= 0) & (steps <= N_BACK)
    stepsf = steps.astype(F32)

    def scores(p, dil, base, bias, with_prev):
        qb = q_ref[0, pl.ds(base, blk, stride=dil), :].astype(BF16)
        if with_prev:
            kb = k_ref[0, pl.ds(base - blk * dil, 2 * blk, stride=dil), :].astype(BF16)
            sc = _dot_nt(qb, kb) + bias
        else:
            kb = k_ref[0, pl.ds(base, blk, stride=dil), :].astype(BF16)
            sc = _dot_nt(qb, kb) + bias[:, blk:]
        return sc

    def finish(sc, p, dil, base, with_prev):
        if with_prev:
            vb = v_ref[0, pl.ds(base - blk * dil, 2 * blk, stride=dil), :].astype(BF16)
        else:
            vb = v_ref[0, pl.ds(base, blk, stride=dil), :].astype(BF16)
        m = jnp.max(sc, axis=-1, keepdims=True)
        pr = jnp.exp(sc - m).astype(BF16)
        res = _dot(pr, jnp.concatenate([vb, jnp.ones_like(vb)], axis=1))
        o_s[p, pl.ds(base, blk, stride=dil), :] = res[:, :HEAD]
        d_s[p, pl.ds(base, blk, stride=dil), :] = res[:, HEAD:]
        m_s[p, pl.ds(base, blk, stride=dil), :] = jnp.broadcast_to(m, (blk, HEAD))

    def bias_for(dil):
        return jnp.where(ok, -slope[:, 0:1] * (stepsf * float(dil)), -jnp.inf)

    biases = [bias_for(d) for d in DILATIONS]
    blocks = []
    for p, dil in enumerate(DILATIONS):
        for n in range(seq // (dil * blk)):
            for r in range(dil):
                blocks.append((p, dil, r + n * dil * blk, n > 0))

    pending = []
    for p, dil, base, with_prev in blocks:
        pending.append((scores(p, dil, base, biases[p], with_prev), p, dil, base, with_prev))
        if len(pending) > ATT_LOOKAHEAD:
            finish(*pending.pop(0))
    for item in pending:
        finish(*item)

    def merge_body(i, carry):
        rows = pl.ds(pl.multiple_of(i * rows_n, rows_n), rows_n)
        ms = [m_s[p, rows, :] for p in range(npat)]
        m = functools.reduce(jnp.maximum, ms)
        es = [jnp.exp(mp - m) for mp in ms]
        den = functools.reduce(lambda a, b: a + b, [e * d_s[p, rows, :] for p, e in enumerate(es)])
        num = functools.reduce(lambda a, b: a + b, [e * o_s[p, rows, :] for p, e in enumerate(es)])
        o_ref[0, rows, :] = (num / den).astype(o_ref.dtype)
        return carry

    lax.fori_loop(0, seq // rows_n, merge_body, 0)


def _attention(proj, heads, col0):
    b, s, _ = proj.shape
    slopes = jnp.exp2(-8.0 * jnp.arange(1, heads + 1, dtype=F32) / heads)
    slopes = jnp.broadcast_to(slopes[:, None, None], (heads, 1, HEAD))

    def col(group):
        return pl.BlockSpec((1, s, HEAD), lambda i, j, group=group: (i, 0, col0 + group * heads + j))

    npat = len(DILATIONS)
    return pl.pallas_call(
        functools.partial(_attn_kernel, seq=s),
        grid=(b, heads),
        in_specs=[col(0), col(1), col(2),
                  pl.BlockSpec((1, 1, HEAD), lambda i, j: (j, 0, 0))],
        out_specs=pl.BlockSpec((1, s, HEAD), lambda i, j: (i, 0, j)),
        out_shape=jax.ShapeDtypeStruct((b, s, heads * HEAD), BF16),
        scratch_shapes=[pltpu.VMEM((npat, s, HEAD), F32), pltpu.VMEM((npat, s, HEAD), F32),
                        pltpu.VMEM((npat, s, HEAD), F32)],
        compiler_params=pltpu.CompilerParams(vmem_limit_bytes=VMEM_LIMIT),
        name="dilated_attn",
    )(proj, proj, proj, slopes)


def _outproj_norm_kernel(a_ref, b_ref, w_ref, x_ref, gate_ref, nw_ref, scale_ref, shift_ref, x1_ref, h_ref):
    ka = a_ref.shape[1]
    mix = _dot(a_ref[...], w_ref[0:ka, :]) + _dot(b_ref[...], w_ref[ka:, :])
    x1 = x_ref[...] + gate_ref[0] * mix
    x1_ref[...] = x1
    ms = jnp.mean(x1 * x1, axis=-1, keepdims=True)
    y = x1 * lax.rsqrt(ms + EPS) * nw_ref[...]
    h_ref[...] = (y * (1.0 + scale_ref[0]) + shift_ref[0]).astype(h_ref.dtype)


def _out_projection_norm(a, b, w, x, gate, norm_w, scale, shift, seq, tm=512):
    m, ka = a.shape
    kb = b.shape[1]
    n = w.shape[1]
    per_seq = seq // tm

    def per_batch():
        return pl.BlockSpec((1, 1, n), lambda i: (i // per_seq, 0, 0))

    return pl.pallas_call(
        _outproj_norm_kernel,
        grid=(m // tm,),
        in_specs=[pl.BlockSpec((tm, ka), lambda i: (i, 0)),
                  pl.BlockSpec((tm, kb), lambda i: (i, 0)),
                  pl.BlockSpec((ka + kb, n), lambda i: (0, 0)),
                  pl.BlockSpec((tm, n), lambda i: (i, 0)),
                  per_batch(),
                  pl.BlockSpec((1, n), lambda i: (0, 0)),
                  per_batch(), per_batch()],
        out_specs=[pl.BlockSpec((tm, n), lambda i: (i, 0)),
                   pl.BlockSpec((tm, n), lambda i: (i, 0))],
        out_shape=[jax.ShapeDtypeStruct((m, n), F32), jax.ShapeDtypeStruct((m, n), BF16)],
        compiler_params=pltpu.CompilerParams(vmem_limit_bytes=VMEM_LIMIT),
        name="out_proj_norm",
    )(a, b, w, x, gate, norm_w, scale, shift)


def _up_kernel(h_ref, wa_ref, wg_ref, cw_ref, cb_ref, y_ref, wa_s, wg_s, halo_s, *, per_seq):
    i = pl.program_id(1)

    @pl.when(i == 0)
    def _():
        wa_s[...] = wa_ref[...].astype(BF16)
        wg_s[...] = wg_ref[...].astype(BF16)

    @pl.when(i % per_seq == 0)
    def _():
        halo_s[...] = jnp.zeros_like(halo_s)

    h = h_ref[...]
    a = _dot(h, wa_s[...])
    g = _dot(h, wg_s[...])
    tm = a.shape[0]
    prev = halo_s[...]
    p1 = prev[7:8, :]
    p2 = prev[6:7, :]
    row = lax.broadcasted_iota(jnp.int32, a.shape, 0)
    a1 = jnp.where(row == 0, p1, pltpu.roll(a, 1, axis=0))
    a2 = jnp.where(row == 0, p2, jnp.where(row == 1, p1, pltpu.roll(a, 2, axis=0)))
    halo_s[...] = a[tm - 8:tm, :]
    cw = cw_ref[...]
    y = cb_ref[...] + a2 * cw[0:1, :] + a1 * cw[1:2, :] + a * cw[2:3, :]
    y_ref[...] = (_silu(y) * g).astype(y_ref.dtype)


def _up_projection(h, w, conv_w, conv_b, seq, tm=1024, tn=512):
    m, k = h.shape
    dff = w.shape[1] // 2
    nb = dff // tn
    return pl.pallas_call(
        functools.partial(_up_kernel, per_seq=seq // tm),
        grid=(nb, m // tm),
        in_specs=[pl.BlockSpec((tm, k), lambda j, i: (i, 0)),
                  pl.BlockSpec((k, tn), lambda j, i: (0, j)),
                  pl.BlockSpec((k, tn), lambda j, i: (0, nb + j)),
                  pl.BlockSpec((CONV_WIDTH, tn), lambda j, i: (0, j)),
                  pl.BlockSpec((1, tn), lambda j, i: (0, j))],
        out_specs=pl.BlockSpec((tm, tn), lambda j, i: (i, j)),
        out_shape=jax.ShapeDtypeStruct((m, dff), BF16),
        scratch_shapes=[pltpu.VMEM((k, tn), BF16), pltpu.VMEM((k, tn), BF16), pltpu.VMEM((8, tn), F32)],
        compiler_params=pltpu.CompilerParams(
            dimension_semantics=("arbitrary", "arbitrary"), vmem_limit_bytes=VMEM_LIMIT),
        name="up_proj_conv_gate",
    )(h, w, w, conv_w, conv_b)


def _down_kernel(y_ref, w_ref, x_ref, gate_ref, o_ref, w_s):
    @pl.when(pl.program_id(1) == 0)
    def _():
        w_s[...] = w_ref[...].astype(BF16)

    o_ref[...] = x_ref[...] + gate_ref[0] * _dot(y_ref[...], w_s[...])


def _down_projection(y, w, x, gate, seq, tm=512, tn=512):
    m, k = y.shape
    n = w.shape[1]
    per_seq = seq // tm
    return pl.pallas_call(
        _down_kernel,
        grid=(n // tn, m // tm),
        in_specs=[pl.BlockSpec((tm, k), lambda j, i: (i, 0)),
                  pl.BlockSpec((k, tn), lambda j, i: (0, j)),
                  pl.BlockSpec((tm, tn), lambda j, i: (i, j)),
                  pl.BlockSpec((1, 1, tn), lambda j, i: (i // per_seq, 0, j))],
        out_specs=pl.BlockSpec((tm, tn), lambda j, i: (i, j)),
        out_shape=jax.ShapeDtypeStruct((m, n), F32),
        scratch_shapes=[pltpu.VMEM((k, tn), BF16)],
        compiler_params=pltpu.CompilerParams(
            dimension_semantics=("arbitrary", "arbitrary"), vmem_limit_bytes=VMEM_LIMIT),
        name="down_proj",
    )(y, w, x, gate)


def _layer(layer, x, mod, norm1_w, w_in, lb_logits, hg_norm_w, q_norm_w, k_norm_w, w_out, norm2_w, w_up, conv_w,
           conv_b, w_down):
    b, s, d = x.shape
    shift1, scale1, gate1, shift2, scale2, gate2 = [mod[:b, j * d:(j + 1) * d].reshape(b, 1, d) for j in range(6)]

    h = _norm_modulate(x, norm1_w.reshape(1, d), scale1, shift1)
    tn = ATT_HEADS * HEAD
    q_tile = 4 * HG_HEADS * HEAD // tn
    n_tiles = w_in.shape[1] // tn
    head_w = jnp.ones((n_tiles, 1, tn), F32)
    head_w = head_w.at[q_tile, 0].set(jnp.tile(q_norm_w * (HEAD ** -0.5), ATT_HEADS))
    head_w = head_w.at[q_tile + 1, 0].set(jnp.tile(k_norm_w, ATT_HEADS))
    proj = _in_projection(h.reshape(b * s, d), w_in, head_w, (q_tile, q_tile + 1), tn=tn).reshape(b, s, -1)
    a_out = _hgrn(proj, lb_logits, hg_norm_w.reshape(1, HEAD), HG_HEADS, layer)
    b_out = _attention(proj, ATT_HEADS, 4 * HG_HEADS)
    x1, h2 = _out_projection_norm(a_out.reshape(b * s, -1), b_out.reshape(b * s, -1), w_out.astype(BF16),
                                  x.reshape(b * s, d), gate1, norm2_w.reshape(1, d), scale2, shift2, s)
    y = _up_projection(h2, w_up, conv_w, conv_b.reshape(1, -1), s)
    x2 = _down_projection(y, w_down, x1, gate2, s)
    return x2.reshape(b, s, d)


def kernel(x, c, w_ada, b_ada, norm1_w, w_in, lb_logits, hg_norm_w, q_norm_w, k_norm_w, w_out, norm2_w, w_up, conv_w, conv_b, w_down):
    depth = w_ada.shape[0]
    b = x.shape[0]
    c_pad = jnp.pad(c, ((0, 8 - b), (0, 0)))
    for l in range(depth):
        mod = _modulation(c_pad, w_ada[l], b_ada[l].reshape(1, -1))
        x = _layer(l, x, mod, norm1_w[l], w_in[l], lb_logits, hg_norm_w[l], q_norm_w[l], k_norm_w[l],
                   w_out[l], norm2_w[l], w_up[l], conv_w[l], conv_b[l], w_down[l])
    return x
```

```python
import functools

import numpy as np
import jax
import jax.numpy as jnp
from jax import lax
from jax.experimental import pallas as pl
from jax.experimental.pallas import tpu as pltpu

F32 = jnp.float32
BF16 = jnp.bfloat16
EPS = 1e-6
LOG2E = 1.4426950408889634

HEAD = 128
SUB = 8
HG_HEADS = 8
ATT_HEADS = 8
N_BACK = 128
DILATIONS = (1, 4, 16)
CONV_WIDTH = 3
HG_CHUNK = 64
HG_UNROLL = 4
HG_LOOKAHEAD = 5
ATT_LOOKAHEAD = 1
VMEM_LIMIT = 56 * 1024 * 1024


def _dot(a, b):
    return jnp.dot(a, b, preferred_element_type=F32)


def _dot_nt(a, b):
    return lax.dot_general(a, b, (((1,), (1,)), ((), ())), preferred_element_type=F32)


def _dot_tn(a, b):
    return lax.dot_general(a, b, (((0,), (0,)), ((), ())), preferred_element_type=F32)


def _silu(v):
    return v * jax.nn.sigmoid(v)


def _mod_kernel(c_ref, w_ref, b_ref, o_ref):
    ca = _silu(c_ref[...]).astype(BF16)
    o_ref[...] = _dot(ca, w_ref[...].astype(BF16)) + b_ref[...]


def _modulation(c_pad, w_ada, b_ada, tn=1024):
    rows, d = c_pad.shape
    n = w_ada.shape[1]
    return pl.pallas_call(
        _mod_kernel,
        grid=(n // tn,),
        in_specs=[pl.BlockSpec((rows, d), lambda j: (0, 0)),
                  pl.BlockSpec((d, tn), lambda j: (0, j)),
                  pl.BlockSpec((1, tn), lambda j: (0, j))],
        out_specs=pl.BlockSpec((rows, tn), lambda j: (0, j)),
        out_shape=jax.ShapeDtypeStruct((rows, n), F32),
        compiler_params=pltpu.CompilerParams(vmem_limit_bytes=VMEM_LIMIT),
        name="adaln_mod",
    )(c_pad, w_ada, b_ada)


def _norm_kernel(x_ref, w_ref, scale_ref, shift_ref, o_ref):
    x = x_ref[0]
    ms = jnp.mean(x * x, axis=-1, keepdims=True)
    y = x * lax.rsqrt(ms + EPS) * w_ref[...]
    o_ref[0] = (y * (1.0 + scale_ref[0]) + shift_ref[0]).astype(o_ref.dtype)


def _norm_modulate(x, w, scale, shift, ts=512):
    b, s, d = x.shape
    return pl.pallas_call(
        _norm_kernel,
        grid=(b, s // ts),
        in_specs=[pl.BlockSpec((1, ts, d), lambda i, j: (i, j, 0)),
                  pl.BlockSpec((1, d), lambda i, j: (0, 0)),
                  pl.BlockSpec((1, 1, d), lambda i, j: (i, 0, 0)),
                  pl.BlockSpec((1, 1, d), lambda i, j: (i, 0, 0))],
        out_specs=pl.BlockSpec((1, ts, d), lambda i, j: (i, j, 0)),
        out_shape=jax.ShapeDtypeStruct((b, s, d), BF16),
        compiler_params=pltpu.CompilerParams(vmem_limit_bytes=VMEM_LIMIT),
        name="norm_modulate",
    )(x, w, scale, shift)


def _inproj_kernel(x_ref, w_ref, nw_ref, o_ref, wb_ref, *, norm_tiles):
    j = pl.program_id(0)

    @pl.when(pl.program_id(1) == 0)
    def _():
        wb_ref[...] = w_ref[...].astype(BF16)

    o_ref[...] = _dot(x_ref[...], wb_ref[...])

    @pl.when(functools.reduce(jnp.logical_or, [j == t for t in norm_tiles]))
    def _():
        nw = nw_ref[0]
        for c0 in range(0, o_ref.shape[1], HEAD):
            seg = o_ref[:, c0:c0 + HEAD]
            ms = jnp.mean(seg * seg, axis=-1, keepdims=True)
            o_ref[:, c0:c0 + HEAD] = seg * lax.rsqrt(ms + EPS) * nw[:, c0:c0 + HEAD]


def _in_projection(h, w, norm_w, norm_tiles, tm=1024, tn=1024):
    m, k = h.shape
    n = w.shape[1]
    return pl.pallas_call(
        functools.partial(_inproj_kernel, norm_tiles=norm_tiles),
        grid=(n // tn, m // tm),
        in_specs=[pl.BlockSpec((tm, k), lambda j, i: (i, 0)),
                  pl.BlockSpec((k, tn), lambda j, i: (0, j)),
                  pl.BlockSpec((1, 1, tn), lambda j, i: (j, 0, 0))],
        out_specs=pl.BlockSpec((tm, tn), lambda j, i: (i, j)),
        out_shape=jax.ShapeDtypeStruct((m, n), F32),
        scratch_shapes=[pltpu.VMEM((k, tn), BF16)],
        compiler_params=pltpu.CompilerParams(
            dimension_semantics=("arbitrary", "arbitrary"), vmem_limit_bytes=VMEM_LIMIT),
        name="in_proj",
    )(h, w, norm_w)


def _hgrn_tables(c):
    t = np.arange(c)[:, None]
    u = np.arange(c)[None, :]
    masks = []
    m = c
    while m >= 2:
        half = m // 2
        start = (t // m) * m
        u_start = (u // m) * m
        masks.append(((t - start) >= half) & ((u - u_start) < half) & (u_start == start))
        m //= 2
    cum = (u <= t).astype(np.float32)
    return np.concatenate([cum, cum, cum], axis=1), np.stack(masks).astype(np.float32)


def _level_signs(c, width):
    t = lax.broadcasted_iota(jnp.int32, (c, width), 0)
    signs = []
    m = c
    while m >= 8:
        signs.append(jnp.where(t % m >= m // 2, 1.0, -1.0).astype(F32))
        m //= 2
    return signs


def _level_log_decays(g, cum, signs):
    c = g.shape[0]
    out = []
    m = c
    for sgn in signs:
        half = m // 2
        mids = [jnp.broadcast_to(cum[j * m + half - 1:j * m + half, :], (m, g.shape[1])) for j in range(c // m)]
        out.append((cum - jnp.concatenate(mids, axis=0)) * sgn)
        m //= 2
    pos = lax.broadcasted_iota(jnp.int32, g.shape, 0) % 4
    g_next = pltpu.roll(g, c - 1, axis=0)
    g_prev = pltpu.roll(g, 1, axis=0)
    out.append(jnp.where(pos == 0, g_next, jnp.where(pos == 1, 0.0, jnp.where(pos == 2, g, g + g_prev))))
    return out


def _hgrn_kernel(hq_ref, hf_ref, hi_ref, hg_ref, lbl_ref, nw_ref, tab_ref, msk_ref, o_ref, st_ref, *,
                 chunk, heads, layer):
    c = chunk
    assert chunk >= 8 and 2 ** msk_ref.shape[0] == chunk
    ts = hq_ref.shape[1]

    @pl.when(pl.program_id(1) == 0)
    def _():
        st_ref[...] = jnp.zeros_like(st_ref)

    lbl = lbl_ref[...]
    ex = jnp.exp(lbl - jnp.max(lbl, axis=0, keepdims=True))
    lb_all = jnp.sum(ex[0:layer + 1], axis=0, keepdims=True) / jnp.sum(ex, axis=0, keepdims=True)
    nw = nw_ref[...]
    t_idx = lax.broadcasted_iota(jnp.int32, (c, c), 0)
    s_idx = lax.broadcasted_iota(jnp.int32, (c, c), 1)
    near_diag = (s_idx == t_idx) | ((t_idx % 2 == 1) & (s_idx == t_idx - 1))
    odd = lax.broadcasted_iota(jnp.int32, (c, HEAD), 0) % 2 == 1
    signs = _level_signs(c, HEAD)

    def gates(rows, h):
        cols = slice(h * HEAD, (h + 1) * HEAD)
        lb = lb_all[:, cols]
        q = _silu(hq_ref[0, rows, cols])
        f = lb + (1.0 - lb) * jax.nn.sigmoid(hf_ref[0, rows, cols])
        g = jnp.log(f) * LOG2E
        g1 = g.astype(BF16)
        r1 = g - g1.astype(F32)
        g2 = r1.astype(BF16)
        g3 = (r1 - g2.astype(F32)).astype(BF16)
        cum = _dot(tab_ref[...], jnp.concatenate([g1, g2, g3], axis=0))
        return rows, h, q, f, g, cum

    def pair_products(rows, h, q, f, g, cum):
        cols = slice(h * HEAD, (h + 1) * HEAD)
        kk = 1.0 - f
        v = hi_ref[0, rows, cols].astype(BF16)
        e_cum = jnp.exp2(cum)
        e_last = jnp.exp2(cum[c - 1:c, :] - cum)

        state_t = st_ref[h]
        o = _dot_nt((q * e_cum).astype(BF16), state_t.astype(BF16))
        k_dec = (kk * e_last).astype(BF16)
        st_ref[h] = state_t * e_cum[c - 1:c, :] + _dot_tn(v, k_dec)

        e2 = jnp.where(odd, f, 1.0)
        lhs = jnp.concatenate([q * e2, jnp.where(odd, q * (1.0 - f * f), 0.0)], axis=1).astype(BF16)
        rhs = jnp.concatenate([kk * e2, jnp.where(odd, kk, 0.0)], axis=1).astype(BF16)
        a = jnp.where(near_diag, _dot_nt(lhs, rhs), 0.0)
        nb = c // SUB
        blocks = [a[i * SUB:(i + 1) * SUB] for i in range(nb)]
        m = c
        for l, d in enumerate(_level_log_decays(g, cum, signs)):
            e = jnp.exp2(d)
            if m >= 2 * SUB:
                ups = [i for i in range(nb) if (i * SUB) % m >= m // 2]
                tile = lambda x, i: x[i * SUB:(i + 1) * SUB]
                lhs = jnp.concatenate([tile(q, i) * tile(e, i) for i in ups], axis=0)
                rhs = jnp.concatenate([tile(kk, i) if i in ups else tile(kk, i) * tile(e, i) for i in range(nb)],
                                      axis=0)
            else:
                ups = list(range(nb))
                lhs, rhs = q * e, kk * e
            sc = _dot_nt(lhs.astype(BF16), rhs.astype(BF16))
            for j, i in enumerate(ups):
                keep = msk_ref[l, i * SUB:(i + 1) * SUB, :] > 0.5
                blocks[i] = jnp.where(keep, sc[j * SUB:(j + 1) * SUB], blocks[i])
            m //= 2
        return rows, h, o, jnp.concatenate(blocks, axis=0), v

    def output(rows, h, o, a, v):
        cols = slice(h * HEAD, (h + 1) * HEAD)
        o = o + _dot(a.astype(BF16), v)
        ms = jnp.mean(o * o, axis=-1, keepdims=True)
        on = o * lax.rsqrt(ms + EPS) * nw
        o_ref[0, rows, cols] = (on * _silu(hg_ref[0, rows, cols])).astype(o_ref.dtype)

    def body(i, carry):
        after_gates, after_pairs = [], []
        for u in range(HG_UNROLL):
            rows = pl.ds(pl.multiple_of((i * HG_UNROLL + u) * c, c), c)
            for h in range(heads):
                after_gates.append(gates(rows, h))
                if len(after_gates) > HG_LOOKAHEAD:
                    after_pairs.append(pair_products(*after_gates.pop(0)))
                if len(after_pairs) > HG_LOOKAHEAD:
                    output(*after_pairs.pop(0))
        for item in after_gates:
            after_pairs.append(pair_products(*item))
        for item in after_pairs:
            output(*item)
        return carry

    lax.fori_loop(0, ts // (c * HG_UNROLL), body, 0)


def _hgrn(proj, lb_logits, norm_w, heads, layer, ts=512):
    b, s, _ = proj.shape
    nl = lb_logits.shape[0]
    w = heads * HEAD
    tab3, msk = _hgrn_tables(HG_CHUNK)
    tab3 = jnp.asarray(tab3, BF16)
    msk = jnp.asarray(msk, F32)

    def col(group):
        return pl.BlockSpec((1, ts, w), lambda i, j, group=group: (i, j, group))

    return pl.pallas_call(
        functools.partial(_hgrn_kernel, chunk=HG_CHUNK, heads=heads, layer=layer),
        grid=(b, s // ts),
        in_specs=[col(0), col(1), col(2), col(3),
                  pl.BlockSpec((nl, w), lambda i, j: (0, 0)),
                  pl.BlockSpec((1, HEAD), lambda i, j: (0, 0)),
                  pl.BlockSpec(tab3.shape, lambda i, j: (0, 0)),
                  pl.BlockSpec(msk.shape, lambda i, j: (0, 0, 0))],
        out_specs=pl.BlockSpec((1, ts, w), lambda i, j: (i, j, 0)),
        out_shape=jax.ShapeDtypeStruct((b, s, w), BF16),
        scratch_shapes=[pltpu.VMEM((heads, HEAD, HEAD), F32)],
        compiler_params=pltpu.CompilerParams(
            dimension_semantics=("arbitrary", "arbitrary"), vmem_limit_bytes=VMEM_LIMIT),
        name="hgrn2",
    )(proj, proj, proj, proj, lb_logits, norm_w, tab3, msk)


def _attn_kernel(q_ref, k_ref, v_ref, slope_ref, o_ref, o_s, m_s, d_s, *, seq):
    assert DILATIONS == (1, 4, 16) and seq == 16 * N_BACK
    blk = N_BACK
    npat = len(DILATIONS)
    slope = slope_ref[0]
    rows_n = 256

    qi = lax.broadcasted_iota(jnp.int32, (blk, 2 * blk), 0)
    kj = lax.broadcasted_iota(jnp.int32, (blk, 2 * blk), 1)
    steps = qi + blk - kj
    ok = (steps >= 0) & (steps <= N_BACK)
    stepsf = steps.astype(F32)

    def scores(p, dil, base, bias, with_prev):
        qb = q_ref[0, pl.ds(base, blk, stride=dil), :].astype(BF16)
        if with_prev:
            kb = k_ref[0, pl.ds(base - blk * dil, 2 * blk, stride=dil), :].astype(BF16)
            sc = _dot_nt(qb, kb) + bias
        else:
            kb = k_ref[0, pl.ds(base, blk, stride=dil), :].astype(BF16)
            sc = _dot_nt(qb, kb) + bias[:, blk:]
        return sc

    def softmax(sc, p, dil, base, with_prev):
        m = jnp.max(sc, axis=-1, keepdims=True)
        pr = jnp.exp2(sc - m).astype(BF16)
        m_s[p, pl.ds(base, blk, stride=dil), :] = jnp.broadcast_to(m, (blk, HEAD))
        return pr, p, dil, base, with_prev

    def values(pr, p, dil, base, with_prev):
        if with_prev:
            vb = v_ref[0, pl.ds(base - blk * dil, 2 * blk, stride=dil), :].astype(BF16)
        else:
            vb = v_ref[0, pl.ds(base, blk, stride=dil), :].astype(BF16)
        res = _dot(pr, jnp.concatenate([vb, jnp.ones_like(vb)], axis=1))
        o_s[p, pl.ds(base, blk, stride=dil), :] = res[:, :HEAD]
        d_s[p, pl.ds(base, blk, stride=dil), :] = res[:, HEAD:]

    def bias_for(dil):
        return jnp.where(ok, -slope[:, 0:1] * (stepsf * (float(dil) * LOG2E)), -jnp.inf)

    biases = [bias_for(d) for d in DILATIONS]
    blocks = []
    for p, dil in enumerate(DILATIONS):
        for n in range(seq // (dil * blk)):
            for r in range(dil):
                blocks.append((p, dil, r + n * dil * blk, n > 0))

    scored, probs = [], []
    for p, dil, base, with_prev in blocks:
        scored.append((scores(p, dil, base, biases[p], with_prev), p, dil, base, with_prev))
        if len(scored) > ATT_LOOKAHEAD:
            probs.append(softmax(*scored.pop(0)))
        if len(probs) > ATT_LOOKAHEAD:
            values(*probs.pop(0))
    for item in scored:
        probs.append(softmax(*item))
    for item in probs:
        values(*item)

    def merge_body(i, carry):
        rows = pl.ds(pl.multiple_of(i * rows_n, rows_n), rows_n)
        ms = [m_s[p, rows, :] for p in range(npat)]
        m = functools.reduce(jnp.maximum, ms)
        es = [jnp.exp2(mp - m) for mp in ms]
        den = functools.reduce(lambda a, b: a + b, [e * d_s[p, rows, :] for p, e in enumerate(es)])
        num = functools.reduce(lambda a, b: a + b, [e * o_s[p, rows, :] for p, e in enumerate(es)])
        o_ref[0, rows, :] = (num / den).astype(o_ref.dtype)
        return carry

    lax.fori_loop(0, seq // rows_n, merge_body, 0)


def _attention(proj, heads, col0):
    b, s, _ = proj.shape
    slopes = jnp.exp2(-8.0 * jnp.arange(1, heads + 1, dtype=F32) / heads)
    slopes = jnp.broadcast_to(slopes[:, None, None], (heads, 1, HEAD))

    def col(group):
        return pl.BlockSpec((1, s, HEAD), lambda i, j, group=group: (i, 0, col0 + group * heads + j))

    npat = len(DILATIONS)
    return pl.pallas_call(
        functools.partial(_attn_kernel, seq=s),
        grid=(b, heads),
        in_specs=[col(0), col(1), col(2),
                  pl.BlockSpec((1, 1, HEAD), lambda i, j: (j, 0, 0))],
        out_specs=pl.BlockSpec((1, s, HEAD), lambda i, j: (i, 0, j)),
        out_shape=jax.ShapeDtypeStruct((b, s, heads * HEAD), BF16),
        scratch_shapes=[pltpu.VMEM((npat, s, HEAD), F32), pltpu.VMEM((npat, s, HEAD), F32),
                        pltpu.VMEM((npat, s, HEAD), F32)],
        compiler_params=pltpu.CompilerParams(vmem_limit_bytes=VMEM_LIMIT),
        name="dilated_attn",
    )(proj, proj, proj, slopes)


def _outproj_norm_kernel(a_ref, b_ref, w_ref, x_ref, gate_ref, nw_ref, scale_ref, shift_ref, x1_ref, h_ref):
    ka = a_ref.shape[1]
    mix = _dot(a_ref[...], w_ref[0:ka, :]) + _dot(b_ref[...], w_ref[ka:, :])
    x1 = x_ref[...] + gate_ref[0] * mix
    x1_ref[...] = x1
    ms = jnp.mean(x1 * x1, axis=-1, keepdims=True)
    y = x1 * lax.rsqrt(ms + EPS) * nw_ref[...]
    h_ref[...] = (y * (1.0 + scale_ref[0]) + shift_ref[0]).astype(h_ref.dtype)


def _out_projection_norm(a, b, w, x, gate, norm_w, scale, shift, seq, tm=512):
    m, ka = a.shape
    kb = b.shape[1]
    n = w.shape[1]
    per_seq = seq // tm

    def per_batch():
        return pl.BlockSpec((1, 1, n), lambda i: (i // per_seq, 0, 0))

    return pl.pallas_call(
        _outproj_norm_kernel,
        grid=(m // tm,),
        in_specs=[pl.BlockSpec((tm, ka), lambda i: (i, 0)),
                  pl.BlockSpec((tm, kb), lambda i: (i, 0)),
                  pl.BlockSpec((ka + kb, n), lambda i: (0, 0)),
                  pl.BlockSpec((tm, n), lambda i: (i, 0)),
                  per_batch(),
                  pl.BlockSpec((1, n), lambda i: (0, 0)),
                  per_batch(), per_batch()],
        out_specs=[pl.BlockSpec((tm, n), lambda i: (i, 0)),
                   pl.BlockSpec((tm, n), lambda i: (i, 0))],
        out_shape=[jax.ShapeDtypeStruct((m, n), F32), jax.ShapeDtypeStruct((m, n), BF16)],
        compiler_params=pltpu.CompilerParams(vmem_limit_bytes=VMEM_LIMIT),
        name="out_proj_norm",
    )(a, b, w, x, gate, norm_w, scale, shift)


def _up_kernel(h_ref, wa_ref, wg_ref, cw_ref, cb_ref, y_ref, wa_s, wg_s, halo_s, *, per_seq):
    i = pl.program_id(1)

    @pl.when(i == 0)
    def _():
        wa_s[...] = wa_ref[...].astype(BF16)
        wg_s[...] = wg_ref[...].astype(BF16)

    @pl.when(i % per_seq == 0)
    def _():
        halo_s[...] = jnp.zeros_like(halo_s)

    h = h_ref[...]
    a = _dot(h, wa_s[...])
    g = _dot(h, wg_s[...])
    tm = a.shape[0]
    prev = halo_s[...]
    p1 = prev[7:8, :]
    p2 = prev[6:7, :]
    row = lax.broadcasted_iota(jnp.int32, a.shape, 0)
    a1 = jnp.where(row == 0, p1, pltpu.roll(a, 1, axis=0))
    a2 = jnp.where(row == 0, p2, jnp.where(row == 1, p1, pltpu.roll(a, 2, axis=0)))
    halo_s[...] = a[tm - 8:tm, :]
    cw = cw_ref[...]
    y = cb_ref[...] + a2 * cw[0:1, :] + a1 * cw[1:2, :] + a * cw[2:3, :]
    y_ref[...] = (_silu(y) * g).astype(y_ref.dtype)


def _up_projection(h, w, conv_w, conv_b, seq, tm=1024, tn=512):
    m, k = h.shape
    dff = w.shape[1] // 2
    nb = dff // tn
    return pl.pallas_call(
        functools.partial(_up_kernel, per_seq=seq // tm),
        grid=(nb, m // tm),
        in_specs=[pl.BlockSpec((tm, k), lambda j, i: (i, 0)),
                  pl.BlockSpec((k, tn), lambda j, i: (0, j)),
                  pl.BlockSpec((k, tn), lambda j, i: (0, nb + j)),
                  pl.BlockSpec((CONV_WIDTH, tn), lambda j, i: (0, j)),
                  pl.BlockSpec((1, tn), lambda j, i: (0, j))],
        out_specs=pl.BlockSpec((tm, tn), lambda j, i: (i, j)),
        out_shape=jax.ShapeDtypeStruct((m, dff), BF16),
        scratch_shapes=[pltpu.VMEM((k, tn), BF16), pltpu.VMEM((k, tn), BF16), pltpu.VMEM((8, tn), F32)],
        compiler_params=pltpu.CompilerParams(
            dimension_semantics=("arbitrary", "arbitrary"), vmem_limit_bytes=VMEM_LIMIT),
        name="up_proj_conv_gate",
    )(h, w, w, conv_w, conv_b)


def _down_kernel(y_ref, w_ref, x_ref, gate_ref, o_ref, w_s):
    @pl.when(pl.program_id(1) == 0)
    def _():
        w_s[...] = w_ref[...].astype(BF16)

    o_ref[...] = x_ref[...] + gate_ref[0] * _dot(y_ref[...], w_s[...])


def _down_projection(y, w, x, gate, seq, tm=512, tn=512):
    m, k = y.shape
    n = w.shape[1]
    per_seq = seq // tm
    return pl.pallas_call(
        _down_kernel,
        grid=(n // tn, m // tm),
        in_specs=[pl.BlockSpec((tm, k), lambda j, i: (i, 0)),
                  pl.BlockSpec((k, tn), lambda j, i: (0, j)),
                  pl.BlockSpec((tm, tn), lambda j, i: (i, j)),
                  pl.BlockSpec((1, 1, tn), lambda j, i: (i // per_seq, 0, j))],
        out_specs=pl.BlockSpec((tm, tn), lambda j, i: (i, j)),
        out_shape=jax.ShapeDtypeStruct((m, n), F32),
        scratch_shapes=[pltpu.VMEM((k, tn), BF16)],
        compiler_params=pltpu.CompilerParams(
            dimension_semantics=("arbitrary", "arbitrary"), vmem_limit_bytes=VMEM_LIMIT),
        name="down_proj",
    )(y, w, x, gate)


def _layer(layer, x, mod, norm1_w, w_in, lb_logits, hg_norm_w, q_norm_w, k_norm_w, w_out, norm2_w, w_up, conv_w,
           conv_b, w_down):
    b, s, d = x.shape
    shift1, scale1, gate1, shift2, scale2, gate2 = [mod[:b, j * d:(j + 1) * d].reshape(b, 1, d) for j in range(6)]

    h = _norm_modulate(x, norm1_w.reshape(1, d), scale1, shift1)
    tn = ATT_HEADS * HEAD
    q_tile = 4 * HG_HEADS * HEAD // tn
    n_tiles = w_in.shape[1] // tn
    head_w = jnp.ones((n_tiles, 1, tn), F32)
    head_w = head_w.at[q_tile, 0].set(jnp.tile(q_norm_w * (HEAD ** -0.5 * LOG2E), ATT_HEADS))
    head_w = head_w.at[q_tile + 1, 0].set(jnp.tile(k_norm_w, ATT_HEADS))
    proj = _in_projection(h.reshape(b * s, d), w_in, head_w, (q_tile, q_tile + 1), tn=tn).reshape(b, s, -1)
    a_out = _hgrn(proj, lb_logits, hg_norm_w.reshape(1, HEAD), HG_HEADS, layer)
    b_out = _attention(proj, ATT_HEADS, 4 * HG_HEADS)
    x1, h2 = _out_projection_norm(a_out.reshape(b * s, -1), b_out.reshape(b * s, -1), w_out.astype(BF16),
                                  x.reshape(b * s, d), gate1, norm2_w.reshape(1, d), scale2, shift2, s)
    y = _up_projection(h2, w_up, conv_w, conv_b.reshape(1, -1), s)
    x2 = _down_projection(y, w_down, x1, gate2, s)
    return x2.reshape(b, s, d)


def kernel(x, c, w_ada, b_ada, norm1_w, w_in, lb_logits, hg_norm_w, q_norm_w, k_norm_w, w_out, norm2_w, w_up, conv_w, conv_b, w_down):
    depth = w_ada.shape[0]
    b = x.shape[0]
    c_pad = jnp.pad(c, ((0, 8 - b), (0, 0)))
    for l in range(depth):
        mod = _modulation(c_pad, w_ada[l], b_ada[l].reshape(1, -1))
        x = _layer(l, x, mod, norm1_w[l], w_in[l], lb_logits, hg_norm_w[l], q_norm_w[l], k_norm_w[l],
                   w_out[l], norm2_w[l], w_up[l], conv_w[l], conv_b[l], w_down[l])
    return x
```

```python
import functools

import numpy as np
import jax
import jax.numpy as jnp
from jax import lax
from jax.experimental import pallas as pl
from jax.experimental.pallas import tpu as pltpu

F32 = jnp.float32
BF16 = jnp.bfloat16
EPS = 1e-6
LOG2E = 1.4426950408889634

HEAD = 128
SUB = 8
HG_HEADS = 8
ATT_HEADS = 8
N_BACK = 128
DILATIONS = (1, 4, 16)
CONV_WIDTH = 3
HG_CHUNK = 64
HG_UNROLL = 4
HG_LOOKAHEAD = 5
ATT_LOOKAHEAD = 1
VMEM_LIMIT = 56 * 1024 * 1024


def _dot(a, b):
    return jnp.dot(a, b, preferred_element_type=F32)


def _dot_nt(a, b):
    return lax.dot_general(a, b, (((1,), (1,)), ((), ())), preferred_element_type=F32)


def _dot_tn(a, b):
    return lax.dot_general(a, b, (((0,), (0,)), ((), ())), preferred_element_type=F32)


def _silu(v):
    return v * jax.nn.sigmoid(v)


def _mod_kernel(c_ref, w_ref, b_ref, o_ref):
    ca = _silu(c_ref[...]).astype(BF16)
    o_ref[...] = _dot(ca, w_ref[...].astype(BF16)) + b_ref[...]


def _modulation(c_pad, w_ada, b_ada, tn=1024):
    rows, d = c_pad.shape
    n = w_ada.shape[1]
    return pl.pallas_call(
        _mod_kernel,
        grid=(n // tn,),
        in_specs=[pl.BlockSpec((rows, d), lambda j: (0, 0)),
                  pl.BlockSpec((d, tn), lambda j: (0, j)),
                  pl.BlockSpec((1, tn), lambda j: (0, j))],
        out_specs=pl.BlockSpec((rows, tn), lambda j: (0, j)),
        out_shape=jax.ShapeDtypeStruct((rows, n), F32),
        compiler_params=pltpu.CompilerParams(vmem_limit_bytes=VMEM_LIMIT),
        name="adaln_mod",
    )(c_pad, w_ada, b_ada)


def _norm_kernel(x_ref, w_ref, scale_ref, shift_ref, o_ref):
    x = x_ref[0]
    ms = jnp.mean(x * x, axis=-1, keepdims=True)
    y = x * lax.rsqrt(ms + EPS) * w_ref[...]
    o_ref[0] = (y * (1.0 + scale_ref[0]) + shift_ref[0]).astype(o_ref.dtype)


def _norm_modulate(x, w, scale, shift, ts=512):
    b, s, d = x.shape
    return pl.pallas_call(
        _norm_kernel,
        grid=(b, s // ts),
        in_specs=[pl.BlockSpec((1, ts, d), lambda i, j: (i, j, 0)),
                  pl.BlockSpec((1, d), lambda i, j: (0, 0)),
                  pl.BlockSpec((1, 1, d), lambda i, j: (i, 0, 0)),
                  pl.BlockSpec((1, 1, d), lambda i, j: (i, 0, 0))],
        out_specs=pl.BlockSpec((1, ts, d), lambda i, j: (i, j, 0)),
        out_shape=jax.ShapeDtypeStruct((b, s, d), BF16),
        compiler_params=pltpu.CompilerParams(vmem_limit_bytes=VMEM_LIMIT),
        name="norm_modulate",
    )(x, w, scale, shift)


def _inproj_kernel(x_ref, w_ref, nw_ref, o_ref, wb_ref, *, norm_tiles):
    j = pl.program_id(0)

    @pl.when(pl.program_id(1) == 0)
    def _():
        wb_ref[...] = w_ref[...].astype(BF16)

    o_ref[...] = _dot(x_ref[...], wb_ref[...])

    @pl.when(functools.reduce(jnp.logical_or, [j == t for t in norm_tiles]))
    def _():
        nw = nw_ref[0]
        for c0 in range(0, o_ref.shape[1], HEAD):
            seg = o_ref[:, c0:c0 + HEAD]
            ms = jnp.mean(seg * seg, axis=-1, keepdims=True)
            o_ref[:, c0:c0 + HEAD] = seg * lax.rsqrt(ms + EPS) * nw[:, c0:c0 + HEAD]


def _in_projection(h, w, norm_w, norm_tiles, tm=1024, tn=1024):
    m, k = h.shape
    n = w.shape[1]
    return pl.pallas_call(
        functools.partial(_inproj_kernel, norm_tiles=norm_tiles),
        grid=(n // tn, m // tm),
        in_specs=[pl.BlockSpec((tm, k), lambda j, i: (i, 0)),
                  pl.BlockSpec((k, tn), lambda j, i: (0, j)),
                  pl.BlockSpec((1, 1, tn), lambda j, i: (j, 0, 0))],
        out_specs=pl.BlockSpec((tm, tn), lambda j, i: (i, j)),
        out_shape=jax.ShapeDtypeStruct((m, n), F32),
        scratch_shapes=[pltpu.VMEM((k, tn), BF16)],
        compiler_params=pltpu.CompilerParams(
            dimension_semantics=("arbitrary", "arbitrary"), vmem_limit_bytes=VMEM_LIMIT),
        name="in_proj",
    )(h, w, norm_w)


def _hgrn_tables(c):
    t = np.arange(c)[:, None]
    u = np.arange(c)[None, :]
    masks = []
    m = c
    while m >= 2:
        half = m // 2
        start = (t // m) * m
        u_start = (u // m) * m
        masks.append(((t - start) >= half) & ((u - u_start) < half) & (u_start == start))
        m //= 2
    cum = (u <= t).astype(np.float32)
    return np.concatenate([cum, cum, cum], axis=1), np.stack(masks).astype(np.float32)


def _level_signs(c, width):
    t = lax.broadcasted_iota(jnp.int32, (c, width), 0)
    signs = []
    m = c
    while m >= 8:
        signs.append(jnp.where(t % m >= m // 2, 1.0, -1.0).astype(F32))
        m //= 2
    return signs


def _level_log_decays(g, cum, signs):
    c = g.shape[0]
    out = []
    m = c
    for sgn in signs:
        half = m // 2
        mids = [jnp.broadcast_to(cum[j * m + half - 1:j * m + half, :], (m, g.shape[1])) for j in range(c // m)]
        out.append((cum - jnp.concatenate(mids, axis=0)) * sgn)
        m //= 2
    pos = lax.broadcasted_iota(jnp.int32, g.shape, 0) % 4
    g_next = pltpu.roll(g, c - 1, axis=0)
    g_prev = pltpu.roll(g, 1, axis=0)
    out.append(jnp.where(pos == 0, g_next, jnp.where(pos == 1, 0.0, jnp.where(pos == 2, g, g + g_prev))))
    return out


def _hgrn_kernel(hq_ref, hf_ref, hi_ref, hg_ref, lbl_ref, nw_ref, tab_ref, msk_ref, wn_ref, o_ref, wn_out_ref,
                 st_ref, *, chunk, heads, layer):
    c = chunk
    assert chunk >= 8 and 2 ** msk_ref.shape[0] == chunk
    ts = hq_ref.shape[1]
    wn_out_ref[...] = wn_ref[...].astype(BF16)

    @pl.when(pl.program_id(1) == 0)
    def _():
        st_ref[...] = jnp.zeros_like(st_ref)

    lbl = lbl_ref[...]
    ex = jnp.exp(lbl - jnp.max(lbl, axis=0, keepdims=True))
    lb_all = jnp.sum(ex[0:layer + 1], axis=0, keepdims=True) / jnp.sum(ex, axis=0, keepdims=True)
    nw = nw_ref[...]
    t_idx = lax.broadcasted_iota(jnp.int32, (c, c), 0)
    s_idx = lax.broadcasted_iota(jnp.int32, (c, c), 1)
    near_diag = (s_idx == t_idx) | ((t_idx % 2 == 1) & (s_idx == t_idx - 1))
    odd = lax.broadcasted_iota(jnp.int32, (c, HEAD), 0) % 2 == 1
    signs = _level_signs(c, HEAD)

    def gates(rows, h):
        cols = slice(h * HEAD, (h + 1) * HEAD)
        lb = lb_all[:, cols]
        q = _silu(hq_ref[0, rows, cols])
        f = lb + (1.0 - lb) * jax.nn.sigmoid(hf_ref[0, rows, cols])
        g = jnp.log(f) * LOG2E
        g1 = g.astype(BF16)
        r1 = g - g1.astype(F32)
        g2 = r1.astype(BF16)
        g3 = (r1 - g2.astype(F32)).astype(BF16)
        cum = _dot(tab_ref[...], jnp.concatenate([g1, g2, g3], axis=0))
        return rows, h, q, f, g, cum

    def pair_products(rows, h, q, f, g, cum):
        cols = slice(h * HEAD, (h + 1) * HEAD)
        kk = 1.0 - f
        v = hi_ref[0, rows, cols].astype(BF16)
        e_cum = jnp.exp2(cum)
        e_last = jnp.exp2(cum[c - 1:c, :] - cum)

        state_t = st_ref[h]
        o = _dot_nt((q * e_cum).astype(BF16), state_t.astype(BF16))
        k_dec = (kk * e_last).astype(BF16)
        st_ref[h] = state_t * e_cum[c - 1:c, :] + _dot_tn(v, k_dec)

        e2 = jnp.where(odd, f, 1.0)
        lhs = jnp.concatenate([q * e2, jnp.where(odd, q * (1.0 - f * f), 0.0)], axis=1).astype(BF16)
        rhs = jnp.concatenate([kk * e2, jnp.where(odd, kk, 0.0)], axis=1).astype(BF16)
        a = jnp.where(near_diag, _dot_nt(lhs, rhs), 0.0)
        nb = c // SUB
        blocks = [a[i * SUB:(i + 1) * SUB] for i in range(nb)]
        m = c
        for l, d in enumerate(_level_log_decays(g, cum, signs)):
            e = jnp.exp2(d)
            if m >= 2 * SUB:
                ups = [i for i in range(nb) if (i * SUB) % m >= m // 2]
                tile = lambda x, i: x[i * SUB:(i + 1) * SUB]
                lhs = jnp.concatenate([tile(q, i) * tile(e, i) for i in ups], axis=0)
                rhs = jnp.concatenate([tile(kk, i) if i in ups else tile(kk, i) * tile(e, i) for i in range(nb)],
                                      axis=0)
            else:
                ups = list(range(nb))
                lhs, rhs = q * e, kk * e
            sc = _dot_nt(lhs.astype(BF16), rhs.astype(BF16))
            for j, i in enumerate(ups):
                keep = msk_ref[l, i * SUB:(i + 1) * SUB, :] > 0.5
                blocks[i] = jnp.where(keep, sc[j * SUB:(j + 1) * SUB], blocks[i])
            m //= 2
        return rows, h, o, jnp.concatenate(blocks, axis=0), v

    def output(rows, h, o, a, v):
        cols = slice(h * HEAD, (h + 1) * HEAD)
        o = o + _dot(a.astype(BF16), v)
        ms = jnp.mean(o * o, axis=-1, keepdims=True)
        on = o * lax.rsqrt(ms + EPS) * nw
        o_ref[0, rows, cols] = (on * _silu(hg_ref[0, rows, cols])).astype(o_ref.dtype)

    def body(i, carry):
        after_gates, after_pairs = [], []
        for u in range(HG_UNROLL):
            rows = pl.ds(pl.multiple_of((i * HG_UNROLL + u) * c, c), c)
            for h in range(heads):
                after_gates.append(gates(rows, h))
                if len(after_gates) > HG_LOOKAHEAD:
                    after_pairs.append(pair_products(*after_gates.pop(0)))
                if len(after_pairs) > HG_LOOKAHEAD:
                    output(*after_pairs.pop(0))
        for item in after_gates:
            after_pairs.append(pair_products(*item))
        for item in after_pairs:
            output(*item)
        return carry

    lax.fori_loop(0, ts // (c * HG_UNROLL), body, 0)


def _hgrn(proj, lb_logits, norm_w, w_next, heads, layer, ts=512):
    b, s, _ = proj.shape
    nl = lb_logits.shape[0]
    w = heads * HEAD
    st = s // ts
    kn, nn = w_next.shape
    assert kn % (b * st * 16) == 0
    rows = kn // (b * st)
    tab3, msk = _hgrn_tables(HG_CHUNK)
    tab3 = jnp.asarray(tab3, BF16)
    msk = jnp.asarray(msk, F32)

    def col(group):
        return pl.BlockSpec((1, ts, w), lambda i, j, group=group: (i, j, group))

    return pl.pallas_call(
        functools.partial(_hgrn_kernel, chunk=HG_CHUNK, heads=heads, layer=layer),
        grid=(b, st),
        in_specs=[col(0), col(1), col(2), col(3),
                  pl.BlockSpec((nl, w), lambda i, j: (0, 0)),
                  pl.BlockSpec((1, HEAD), lambda i, j: (0, 0)),
                  pl.BlockSpec(tab3.shape, lambda i, j: (0, 0)),
                  pl.BlockSpec(msk.shape, lambda i, j: (0, 0, 0)),
                  pl.BlockSpec((rows, nn), lambda i, j: (i * st + j, 0))],
        out_specs=[pl.BlockSpec((1, ts, w), lambda i, j: (i, j, 0)),
                   pl.BlockSpec((rows, nn), lambda i, j: (i * st + j, 0))],
        out_shape=[jax.ShapeDtypeStruct((b, s, w), BF16), jax.ShapeDtypeStruct((kn, nn), BF16)],
        scratch_shapes=[pltpu.VMEM((heads, HEAD, HEAD), F32)],
        compiler_params=pltpu.CompilerParams(
            dimension_semantics=("arbitrary", "arbitrary"), vmem_limit_bytes=VMEM_LIMIT),
        name="hgrn2",
    )(proj, proj, proj, proj, lb_logits, norm_w, tab3, msk, w_next)


def _attn_kernel(q_ref, k_ref, v_ref, slope_ref, o_ref, o_s, m_s, d_s, *, seq):
    assert DILATIONS == (1, 4, 16) and seq == 16 * N_BACK
    blk = N_BACK
    npat = len(DILATIONS)
    slope = slope_ref[0]
    rows_n = 256

    qi = lax.broadcasted_iota(jnp.int32, (blk, 2 * blk), 0)
    kj = lax.broadcasted_iota(jnp.int32, (blk, 2 * blk), 1)
    steps = qi + blk - kj
    ok = (steps >= 0) & (steps <= N_BACK)
    stepsf = steps.astype(F32)

    def scores(p, dil, base, bias, with_prev):
        qb = q_ref[0, pl.ds(base, blk, stride=dil), :].astype(BF16)
        if with_prev:
            kb = k_ref[0, pl.ds(base - blk * dil, 2 * blk, stride=dil), :].astype(BF16)
            sc = _dot_nt(qb, kb) + bias
        else:
            kb = k_ref[0, pl.ds(base, blk, stride=dil), :].astype(BF16)
            sc = _dot_nt(qb, kb) + bias[:, blk:]
        return sc

    def softmax(sc, p, dil, base, with_prev):
        m = jnp.max(sc, axis=-1, keepdims=True)
        pr = jnp.exp2(sc - m).astype(BF16)
        m_s[p, pl.ds(base, blk, stride=dil), :] = jnp.broadcast_to(m, (blk, HEAD))
        return pr, p, dil, base, with_prev

    def values(pr, p, dil, base, with_prev):
        if with_prev:
            vb = v_ref[0, pl.ds(base - blk * dil, 2 * blk, stride=dil), :].astype(BF16)
        else:
            vb = v_ref[0, pl.ds(base, blk, stride=dil), :].astype(BF16)
        res = _dot(pr, jnp.concatenate([vb, jnp.ones_like(vb)], axis=1))
        o_s[p, pl.ds(base, blk, stride=dil), :] = res[:, :HEAD]
        d_s[p, pl.ds(base, blk, stride=dil), :] = res[:, HEAD:]

    def bias_for(dil):
        return jnp.where(ok, -slope[:, 0:1] * (stepsf * (float(dil) * LOG2E)), -jnp.inf)

    biases = [bias_for(d) for d in DILATIONS]
    blocks = []
    for p, dil in enumerate(DILATIONS):
        for n in range(seq // (dil * blk)):
            for r in range(dil):
                blocks.append((p, dil, r + n * dil * blk, n > 0))

    scored, probs = [], []
    for p, dil, base, with_prev in blocks:
        scored.append((scores(p, dil, base, biases[p], with_prev), p, dil, base, with_prev))
        if len(scored) > ATT_LOOKAHEAD:
            probs.append(softmax(*scored.pop(0)))
        if len(probs) > ATT_LOOKAHEAD:
            values(*probs.pop(0))
    for item in scored:
        probs.append(softmax(*item))
    for item in probs:
        values(*item)

    def merge_body(i, carry):
        rows = pl.ds(pl.multiple_of(i * rows_n, rows_n), rows_n)
        ms = [m_s[p, rows, :] for p in range(npat)]
        m = functools.reduce(jnp.maximum, ms)
        es = [jnp.exp2(mp - m) for mp in ms]
        den = functools.reduce(lambda a, b: a + b, [e * d_s[p, rows, :] for p, e in enumerate(es)])
        num = functools.reduce(lambda a, b: a + b, [e * o_s[p, rows, :] for p, e in enumerate(es)])
        o_ref[0, rows, :] = (num / den).astype(o_ref.dtype)
        return carry

    lax.fori_loop(0, seq // rows_n, merge_body, 0)


def _attention(proj, heads, col0):
    b, s, _ = proj.shape
    slopes = jnp.exp2(-8.0 * jnp.arange(1, heads + 1, dtype=F32) / heads)
    slopes = jnp.broadcast_to(slopes[:, None, None], (heads, 1, HEAD))

    def col(group):
        return pl.BlockSpec((1, s, HEAD), lambda i, j, group=group: (i, 0, col0 + group * heads + j))

    npat = len(DILATIONS)
    return pl.pallas_call(
        functools.partial(_attn_kernel, seq=s),
        grid=(b, heads),
        in_specs=[col(0), col(1), col(2),
                  pl.BlockSpec((1, 1, HEAD), lambda i, j: (j, 0, 0))],
        out_specs=pl.BlockSpec((1, s, HEAD), lambda i, j: (i, 0, j)),
        out_shape=jax.ShapeDtypeStruct((b, s, heads * HEAD), BF16),
        scratch_shapes=[pltpu.VMEM((npat, s, HEAD), F32), pltpu.VMEM((npat, s, HEAD), F32),
                        pltpu.VMEM((npat, s, HEAD), F32)],
        compiler_params=pltpu.CompilerParams(vmem_limit_bytes=VMEM_LIMIT),
        name="dilated_attn",
    )(proj, proj, proj, slopes)


def _outproj_norm_kernel(a_ref, b_ref, w_ref, x_ref, gate_ref, nw_ref, scale_ref, shift_ref, x1_ref, h_ref):
    ka = a_ref.shape[1]
    mix = _dot(a_ref[...], w_ref[0:ka, :]) + _dot(b_ref[...], w_ref[ka:, :])
    x1 = x_ref[...] + gate_ref[0] * mix
    x1_ref[...] = x1
    ms = jnp.mean(x1 * x1, axis=-1, keepdims=True)
    y = x1 * lax.rsqrt(ms + EPS) * nw_ref[...]
    h_ref[...] = (y * (1.0 + scale_ref[0]) + shift_ref[0]).astype(h_ref.dtype)


def _out_projection_norm(a, b, w, x, gate, norm_w, scale, shift, seq, tm=512):
    m, ka = a.shape
    kb = b.shape[1]
    n = w.shape[1]
    per_seq = seq // tm

    def per_batch():
        return pl.BlockSpec((1, 1, n), lambda i: (i // per_seq, 0, 0))

    return pl.pallas_call(
        _outproj_norm_kernel,
        grid=(m // tm,),
        in_specs=[pl.BlockSpec((tm, ka), lambda i: (i, 0)),
                  pl.BlockSpec((tm, kb), lambda i: (i, 0)),
                  pl.BlockSpec((ka + kb, n), lambda i: (0, 0)),
                  pl.BlockSpec((tm, n), lambda i: (i, 0)),
                  per_batch(),
                  pl.BlockSpec((1, n), lambda i: (0, 0)),
                  per_batch(), per_batch()],
        out_specs=[pl.BlockSpec((tm, n), lambda i: (i, 0)),
                   pl.BlockSpec((tm, n), lambda i: (i, 0))],
        out_shape=[jax.ShapeDtypeStruct((m, n), F32), jax.ShapeDtypeStruct((m, n), BF16)],
        compiler_params=pltpu.CompilerParams(vmem_limit_bytes=VMEM_LIMIT),
        name="out_proj_norm",
    )(a, b, w, x, gate, norm_w, scale, shift)


def _up_kernel(h_ref, wa_ref, wg_ref, cw_ref, cb_ref, wn_ref, y_ref, wn_out_ref, wa_s, wg_s, halo_s, *, per_seq):
    i = pl.program_id(1)
    wn_out_ref[...] = wn_ref[...].astype(BF16)

    @pl.when(i == 0)
    def _():
        wa_s[...] = wa_ref[...].astype(BF16)
        wg_s[...] = wg_ref[...].astype(BF16)

    @pl.when(i % per_seq == 0)
    def _():
        halo_s[...] = jnp.zeros_like(halo_s)

    h = h_ref[...]
    a = _dot(h, wa_s[...])
    g = _dot(h, wg_s[...])
    tm = a.shape[0]
    prev = halo_s[...]
    p1 = prev[7:8, :]
    p2 = prev[6:7, :]
    row = lax.broadcasted_iota(jnp.int32, a.shape, 0)
    a1 = jnp.where(row == 0, p1, pltpu.roll(a, 1, axis=0))
    a2 = jnp.where(row == 0, p2, jnp.where(row == 1, p1, pltpu.roll(a, 2, axis=0)))
    halo_s[...] = a[tm - 8:tm, :]
    cw = cw_ref[...]
    y = cb_ref[...] + a2 * cw[0:1, :] + a1 * cw[1:2, :] + a * cw[2:3, :]
    y_ref[...] = (_silu(y) * g).astype(y_ref.dtype)


def _up_projection(h, w, conv_w, conv_b, w_next, seq, tm=1024, tn=512):
    m, k = h.shape
    dff = w.shape[1] // 2
    nb = dff // tn
    mt = m // tm
    kn, nn = w_next.shape
    assert kn % (nb * mt * 16) == 0
    rows = kn // (nb * mt)
    return pl.pallas_call(
        functools.partial(_up_kernel, per_seq=seq // tm),
        grid=(nb, mt),
        in_specs=[pl.BlockSpec((tm, k), lambda j, i: (i, 0)),
                  pl.BlockSpec((k, tn), lambda j, i: (0, j)),
                  pl.BlockSpec((k, tn), lambda j, i: (0, nb + j)),
                  pl.BlockSpec((CONV_WIDTH, tn), lambda j, i: (0, j)),
                  pl.BlockSpec((1, tn), lambda j, i: (0, j)),
                  pl.BlockSpec((rows, nn), lambda j, i: (j * mt + i, 0))],
        out_specs=[pl.BlockSpec((tm, tn), lambda j, i: (i, j)),
                   pl.BlockSpec((rows, nn), lambda j, i: (j * mt + i, 0))],
        out_shape=[jax.ShapeDtypeStruct((m, dff), BF16), jax.ShapeDtypeStruct((kn, nn), BF16)],
        scratch_shapes=[pltpu.VMEM((k, tn), BF16), pltpu.VMEM((k, tn), BF16), pltpu.VMEM((8, tn), F32)],
        compiler_params=pltpu.CompilerParams(
            dimension_semantics=("arbitrary", "arbitrary"), vmem_limit_bytes=VMEM_LIMIT),
        name="up_proj_conv_gate",
    )(h, w, w, conv_w, conv_b, w_next)


def _down_kernel(y_ref, w_ref, x_ref, gate_ref, o_ref):
    o_ref[...] = x_ref[...] + gate_ref[0] * _dot(y_ref[...], w_ref[...])


def _down_projection(y, w, x, gate, seq, tm=512, tn=1024):
    m, k = y.shape
    n = w.shape[1]
    per_seq = seq // tm
    return pl.pallas_call(
        _down_kernel,
        grid=(n // tn, m // tm),
        in_specs=[pl.BlockSpec((tm, k), lambda j, i: (i, 0)),
                  pl.BlockSpec((k, tn), lambda j, i: (0, j)),
                  pl.BlockSpec((tm, tn), lambda j, i: (i, j)),
                  pl.BlockSpec((1, 1, tn), lambda j, i: (i // per_seq, 0, j))],
        out_specs=pl.BlockSpec((tm, tn), lambda j, i: (i, j)),
        out_shape=jax.ShapeDtypeStruct((m, n), F32),
        compiler_params=pltpu.CompilerParams(
            dimension_semantics=("arbitrary", "arbitrary"), vmem_limit_bytes=VMEM_LIMIT),
        name="down_proj",
    )(y, w, x, gate)


def _layer(layer, x, mod, norm1_w, w_in, lb_logits, hg_norm_w, q_norm_w, k_norm_w, w_out, norm2_w, w_up, conv_w,
           conv_b, w_down):
    b, s, d = x.shape
    shift1, scale1, gate1, shift2, scale2, gate2 = [mod[:b, j * d:(j + 1) * d].reshape(b, 1, d) for j in range(6)]

    h = _norm_modulate(x, norm1_w.reshape(1, d), scale1, shift1)
    tn = ATT_HEADS * HEAD
    q_tile = 4 * HG_HEADS * HEAD // tn
    n_tiles = w_in.shape[1] // tn
    head_w = jnp.ones((n_tiles, 1, tn), F32)
    head_w = head_w.at[q_tile, 0].set(jnp.tile(q_norm_w * (HEAD ** -0.5 * LOG2E), ATT_HEADS))
    head_w = head_w.at[q_tile + 1, 0].set(jnp.tile(k_norm_w, ATT_HEADS))
    proj = _in_projection(h.reshape(b * s, d), w_in, head_w, (q_tile, q_tile + 1), tn=tn).reshape(b, s, -1)
    a_out, w_out_bf16 = _hgrn(proj, lb_logits, hg_norm_w.reshape(1, HEAD), w_out, HG_HEADS, layer)
    b_out = _attention(proj, ATT_HEADS, 4 * HG_HEADS)
    x1, h2 = _out_projection_norm(a_out.reshape(b * s, -1), b_out.reshape(b * s, -1), w_out_bf16,
                                  x.reshape(b * s, d), gate1, norm2_w.reshape(1, d), scale2, shift2, s)
    y, w_down_bf16 = _up_projection(h2, w_up, conv_w, conv_b.reshape(1, -1), w_down, s)
    x2 = _down_projection(y, w_down_bf16, x1, gate2, s)
    return x2.reshape(b, s, d)


def kernel(x, c, w_ada, b_ada, norm1_w, w_in, lb_logits, hg_norm_w, q_norm_w, k_norm_w, w_out, norm2_w, w_up, conv_w, conv_b, w_down):
    depth = w_ada.shape[0]
    b = x.shape[0]
    c_pad = jnp.pad(c, ((0, 8 - b), (0, 0)))
    for l in range(depth):
        mod = _modulation(c_pad, w_ada[l], b_ada[l].reshape(1, -1))
        x = _layer(l, x, mod, norm1_w[l], w_in[l], lb_logits, hg_norm_w[l], q_norm_w[l], k_norm_w[l],
                   w_out[l], norm2_w[l], w_up[l], conv_w[l], conv_b[l], w_down[l])
    return x
```

```python
import functools

import numpy as np
import jax
import jax.numpy as jnp
from jax import lax
from jax.experimental import pallas as pl
from jax.experimental.pallas import tpu as pltpu

F32 = jnp.float32
BF16 = jnp.bfloat16
EPS = 1e-6
LOG2E = 1.4426950408889634

HEAD = 128
SUB = 8
HG_HEADS = 8
ATT_HEADS = 8
N_BACK = 128
DILATIONS = (1, 4, 16)
CONV_WIDTH = 3
HG_CHUNK = 64
HG_UNROLL = 4
HG_LOOKAHEAD = 5
ATT_LOOKAHEAD = 1
VMEM_LIMIT = 56 * 1024 * 1024


def _dot(a, b):
    return jnp.dot(a, b, preferred_element_type=F32)


def _dot_nt(a, b):
    return lax.dot_general(a, b, (((1,), (1,)), ((), ())), preferred_element_type=F32)


def _dot_tn(a, b):
    return lax.dot_general(a, b, (((0,), (0,)), ((), ())), preferred_element_type=F32)


def _silu(v):
    return v * jax.nn.sigmoid(v)


def _mod_kernel(c_ref, w_ref, b_ref, o_ref):
    ca = _silu(c_ref[...]).astype(BF16)
    o_ref[0] = _dot(ca, w_ref[...].astype(BF16)) + b_ref[...]


def _modulation(c_pad, w_ada, b_ada, tn=1024):
    rows, d = c_pad.shape
    n = w_ada.shape[1]
    per = d // tn
    out = pl.pallas_call(
        _mod_kernel,
        grid=(n // tn,),
        in_specs=[pl.BlockSpec((rows, d), lambda j: (0, 0)),
                  pl.BlockSpec((d, tn), lambda j: (0, j)),
                  pl.BlockSpec((1, tn), lambda j: (0, j))],
        out_specs=pl.BlockSpec((1, rows, tn), lambda j: (j // per, 0, j % per)),
        out_shape=jax.ShapeDtypeStruct((n // d, rows, d), F32),
        compiler_params=pltpu.CompilerParams(vmem_limit_bytes=VMEM_LIMIT),
        name="adaln_mod",
    )(c_pad, w_ada, b_ada)
    return out.reshape(n // d * rows, 1, d)


def _norm_kernel(x_ref, w_ref, scale_ref, shift_ref, o_ref):
    x = x_ref[0]
    ms = jnp.mean(x * x, axis=-1, keepdims=True)
    y = x * lax.rsqrt(ms + EPS) * w_ref[...]
    o_ref[0] = (y * (1.0 + scale_ref[0]) + shift_ref[0]).astype(o_ref.dtype)


def _norm_modulate(x, w, mod, scale_row, shift_row, ts=1024):
    b, s, d = x.shape
    return pl.pallas_call(
        _norm_kernel,
        grid=(b, s // ts),
        in_specs=[pl.BlockSpec((1, ts, d), lambda i, j: (i, j, 0)),
                  pl.BlockSpec((1, d), lambda i, j: (0, 0)),
                  pl.BlockSpec((1, 1, d), lambda i, j: (scale_row + i, 0, 0)),
                  pl.BlockSpec((1, 1, d), lambda i, j: (shift_row + i, 0, 0))],
        out_specs=pl.BlockSpec((1, ts, d), lambda i, j: (i, j, 0)),
        out_shape=jax.ShapeDtypeStruct((b, s, d), BF16),
        compiler_params=pltpu.CompilerParams(vmem_limit_bytes=VMEM_LIMIT),
        name="norm_modulate",
    )(x, w, mod, mod)


def _inproj_kernel(x_ref, w_ref, nw_ref, o_ref, wb_ref, *, norm_tiles):
    j = pl.program_id(0)

    @pl.when(pl.program_id(1) == 0)
    def _():
        wb_ref[...] = w_ref[...].astype(BF16)

    normed = functools.reduce(jnp.logical_or, [j == t for t in norm_tiles])

    @pl.when(jnp.logical_not(normed))
    def _():
        o_ref[...] = _dot(x_ref[...], wb_ref[...])

    @pl.when(normed)
    def _():
        nw = nw_ref[0]
        half = o_ref.shape[1] // 2
        accs = [_dot(x_ref[...], wb_ref[:, c0:c0 + half]) for c0 in (0, half)]
        for c0, acc in zip((0, half), accs):
            for c1 in range(0, half, HEAD):
                seg = acc[:, c1:c1 + HEAD]
                ms = jnp.mean(seg * seg, axis=-1, keepdims=True)
                o_ref[:, c0 + c1:c0 + c1 + HEAD] = seg * lax.rsqrt(ms + EPS) * nw[:, c0 + c1:c0 + c1 + HEAD]


def _in_projection(h, w, norm_w, norm_tiles, tm=1024, tn=1024):
    m, k = h.shape
    n = w.shape[1]
    return pl.pallas_call(
        functools.partial(_inproj_kernel, norm_tiles=norm_tiles),
        grid=(n // tn, m // tm),
        in_specs=[pl.BlockSpec((tm, k), lambda j, i: (i, 0)),
                  pl.BlockSpec((k, tn), lambda j, i: (0, j)),
                  pl.BlockSpec((1, 1, tn), lambda j, i: (j, 0, 0))],
        out_specs=pl.BlockSpec((tm, tn), lambda j, i: (i, j)),
        out_shape=jax.ShapeDtypeStruct((m, n), F32),
        scratch_shapes=[pltpu.VMEM((k, tn), BF16)],
        compiler_params=pltpu.CompilerParams(
            dimension_semantics=("arbitrary", "arbitrary"), vmem_limit_bytes=VMEM_LIMIT),
        name="in_proj",
    )(h, w, norm_w)


def _hgrn_tables(c):
    t = np.arange(c)[:, None]
    u = np.arange(c)[None, :]
    masks = []
    m = c
    while m >= 2:
        half = m // 2
        start = (t // m) * m
        u_start = (u // m) * m
        masks.append(((t - start) >= half) & ((u - u_start) < half) & (u_start == start))
        m //= 2
    cum = (u <= t).astype(np.float32)
    return np.concatenate([cum, cum, cum], axis=1), np.stack(masks).astype(np.float32)


def _level_signs(c, width):
    t = lax.broadcasted_iota(jnp.int32, (c, width), 0)
    signs = []
    m = c
    while m >= 8:
        signs.append(jnp.where(t % m >= m // 2, 1.0, -1.0).astype(F32))
        m //= 2
    return signs


def _level_log_decays(g, cum, signs):
    c = g.shape[0]
    out = []
    m = c
    for sgn in signs:
        half = m // 2
        mids = [jnp.broadcast_to(cum[j * m + half - 1:j * m + half, :], (m, g.shape[1])) for j in range(c // m)]
        out.append((cum - jnp.concatenate(mids, axis=0)) * sgn)
        m //= 2
    pos = lax.broadcasted_iota(jnp.int32, g.shape, 0) % 4
    g_next = pltpu.roll(g, c - 1, axis=0)
    g_prev = pltpu.roll(g, 1, axis=0)
    out.append(jnp.where(pos == 0, g_next, jnp.where(pos == 1, 0.0, jnp.where(pos == 2, g, g + g_prev))))
    return out


def _hgrn_kernel(hq_ref, hf_ref, hi_ref, hg_ref, lbl_ref, nw_ref, tab_ref, msk_ref, wn_ref, o_ref, wn_out_ref,
                 st_ref, *, chunk, heads, layer):
    c = chunk
    assert chunk >= 8 and 2 ** msk_ref.shape[0] == chunk
    ts = hq_ref.shape[1]
    wn_out_ref[...] = wn_ref[...].astype(BF16)

    @pl.when(pl.program_id(1) == 0)
    def _():
        st_ref[...] = jnp.zeros_like(st_ref)

    lbl = lbl_ref[...]
    ex = jnp.exp(lbl - jnp.max(lbl, axis=0, keepdims=True))
    lb_all = jnp.sum(ex[0:layer + 1], axis=0, keepdims=True) / jnp.sum(ex, axis=0, keepdims=True)
    nw = nw_ref[...]
    t_idx = lax.broadcasted_iota(jnp.int32, (c, c), 0)
    s_idx = lax.broadcasted_iota(jnp.int32, (c, c), 1)
    near_diag = (s_idx == t_idx) | ((t_idx % 2 == 1) & (s_idx == t_idx - 1))
    odd = lax.broadcasted_iota(jnp.int32, (c, HEAD), 0) % 2 == 1
    signs = _level_signs(c, HEAD)

    def gates(rows, h):
        cols = slice(h * HEAD, (h + 1) * HEAD)
        lb = lb_all[:, cols]
        q = _silu(hq_ref[0, rows, cols])
        f = lb + (1.0 - lb) * jax.nn.sigmoid(hf_ref[0, rows, cols])
        g = jnp.log(f) * LOG2E
        g1 = g.astype(BF16)
        r1 = g - g1.astype(F32)
        g2 = r1.astype(BF16)
        g3 = (r1 - g2.astype(F32)).astype(BF16)
        cum = _dot(tab_ref[...], jnp.concatenate([g1, g2, g3], axis=0))
        return rows, h, q, f, g, cum

    def pair_products(rows, h, q, f, g, cum):
        cols = slice(h * HEAD, (h + 1) * HEAD)
        kk = 1.0 - f
        v = hi_ref[0, rows, cols].astype(BF16)
        e_cum = jnp.exp2(cum)
        e_last = jnp.exp2(cum[c - 1:c, :] - cum)

        state_t = st_ref[h]
        o = _dot_nt((q * e_cum).astype(BF16), state_t.astype(BF16))
        k_dec = (kk * e_last).astype(BF16)
        st_ref[h] = state_t * e_cum[c - 1:c, :] + _dot_tn(v, k_dec)

        e2 = jnp.where(odd, f, 1.0)
        lhs = jnp.concatenate([q * e2, jnp.where(odd, q * (1.0 - f * f), 0.0)], axis=1).astype(BF16)
        rhs = jnp.concatenate([kk * e2, jnp.where(odd, kk, 0.0)], axis=1).astype(BF16)
        a = jnp.where(near_diag, _dot_nt(lhs, rhs), 0.0)
        nb = c // SUB
        blocks = [a[i * SUB:(i + 1) * SUB] for i in range(nb)]
        m = c
        for l, d in enumerate(_level_log_decays(g, cum, signs)):
            e = jnp.exp2(d)
            if m >= 2 * SUB:
                ups = [i for i in range(nb) if (i * SUB) % m >= m // 2]
                tile = lambda x, i: x[i * SUB:(i + 1) * SUB]
                lhs = jnp.concatenate([tile(q, i) * tile(e, i) for i in ups], axis=0)
                rhs = jnp.concatenate([tile(kk, i) if i in ups else tile(kk, i) * tile(e, i) for i in range(nb)],
                                      axis=0)
            else:
                ups = list(range(nb))
                lhs, rhs = q * e, kk * e
            sc = _dot_nt(lhs.astype(BF16), rhs.astype(BF16))
            for j, i in enumerate(ups):
                keep = msk_ref[l, i * SUB:(i + 1) * SUB, :] > 0.5
                blocks[i] = jnp.where(keep, sc[j * SUB:(j + 1) * SUB], blocks[i])
            m //= 2
        return rows, h, o, jnp.concatenate(blocks, axis=0), v

    def output(rows, h, o, a, v):
        cols = slice(h * HEAD, (h + 1) * HEAD)
        o = o + _dot(a.astype(BF16), v)
        ms = jnp.mean(o * o, axis=-1, keepdims=True)
        on = o * lax.rsqrt(ms + EPS) * nw
        o_ref[0, rows, cols] = (on * _silu(hg_ref[0, rows, cols])).astype(o_ref.dtype)

    def body(i, carry):
        after_gates, after_pairs = [], []
        for u in range(HG_UNROLL):
            rows = pl.ds(pl.multiple_of((i * HG_UNROLL + u) * c, c), c)
            for h in range(heads):
                after_gates.append(gates(rows, h))
                if len(after_gates) > HG_LOOKAHEAD:
                    after_pairs.append(pair_products(*after_gates.pop(0)))
                if len(after_pairs) > HG_LOOKAHEAD:
                    output(*after_pairs.pop(0))
        for item in after_gates:
            after_pairs.append(pair_products(*item))
        for item in after_pairs:
            output(*item)
        return carry

    lax.fori_loop(0, ts // (c * HG_UNROLL), body, 0)


def _hgrn(proj, lb_logits, norm_w, w_next, heads, layer, ts=512):
    b, s, _ = proj.shape
    nl = lb_logits.shape[0]
    w = heads * HEAD
    st = s // ts
    kn, nn = w_next.shape
    assert kn % (b * st * 16) == 0
    rows = kn // (b * st)
    tab3, msk = _hgrn_tables(HG_CHUNK)
    tab3 = jnp.asarray(tab3, BF16)
    msk = jnp.asarray(msk, F32)

    def col(group):
        return pl.BlockSpec((1, ts, w), lambda i, j, group=group: (i, j, group))

    return pl.pallas_call(
        functools.partial(_hgrn_kernel, chunk=HG_CHUNK, heads=heads, layer=layer),
        grid=(b, st),
        in_specs=[col(0), col(1), col(2), col(3),
                  pl.BlockSpec((nl, w), lambda i, j: (0, 0)),
                  pl.BlockSpec((1, HEAD), lambda i, j: (0, 0)),
                  pl.BlockSpec(tab3.shape, lambda i, j: (0, 0)),
                  pl.BlockSpec(msk.shape, lambda i, j: (0, 0, 0)),
                  pl.BlockSpec((rows, nn), lambda i, j: (i * st + j, 0))],
        out_specs=[pl.BlockSpec((1, ts, w), lambda i, j: (i, j, 0)),
                   pl.BlockSpec((rows, nn), lambda i, j: (i * st + j, 0))],
        out_shape=[jax.ShapeDtypeStruct((b, s, w), BF16), jax.ShapeDtypeStruct((kn, nn), BF16)],
        scratch_shapes=[pltpu.VMEM((heads, HEAD, HEAD), F32)],
        compiler_params=pltpu.CompilerParams(
            dimension_semantics=("arbitrary", "arbitrary"), vmem_limit_bytes=VMEM_LIMIT),
        name="hgrn2",
    )(proj, proj, proj, proj, lb_logits, norm_w, tab3, msk, w_next)


def _attn_kernel(q_ref, k_ref, v_ref, slope_ref, o_ref, o_s, m_s, d_s, *, seq):
    assert DILATIONS == (1, 4, 16) and seq == 16 * N_BACK
    blk = N_BACK
    npat = len(DILATIONS)
    slope = slope_ref[0]
    rows_n = 256

    qi = lax.broadcasted_iota(jnp.int32, (blk, 2 * blk), 0)
    kj = lax.broadcasted_iota(jnp.int32, (blk, 2 * blk), 1)
    steps = qi + blk - kj
    ok = (steps >= 0) & (steps <= N_BACK)
    stepsf = steps.astype(F32)

    def scores(p, dil, base, bias, with_prev):
        qb = q_ref[0, pl.ds(base, blk, stride=dil), :].astype(BF16)
        if with_prev:
            kb = k_ref[0, pl.ds(base - blk * dil, 2 * blk, stride=dil), :].astype(BF16)
            sc = _dot_nt(qb, kb) + bias
        else:
            kb = k_ref[0, pl.ds(base, blk, stride=dil), :].astype(BF16)
            sc = _dot_nt(qb, kb) + bias[:, blk:]
        return sc

    def softmax(sc, p, dil, base, with_prev):
        m = jnp.max(sc, axis=-1, keepdims=True)
        pr = jnp.exp2(sc - m).astype(BF16)
        m_s[p, pl.ds(base, blk, stride=dil), :] = jnp.broadcast_to(m, (blk, HEAD))
        return pr, p, dil, base, with_prev

    def values(pr, p, dil, base, with_prev):
        if with_prev:
            vb = v_ref[0, pl.ds(base - blk * dil, 2 * blk, stride=dil), :].astype(BF16)
        else:
            vb = v_ref[0, pl.ds(base, blk, stride=dil), :].astype(BF16)
        res = _dot(pr, jnp.concatenate([vb, jnp.ones_like(vb)], axis=1))
        o_s[p, pl.ds(base, blk, stride=dil), :] = res[:, :HEAD]
        d_s[p, pl.ds(base, blk, stride=dil), :] = res[:, HEAD:]

    def bias_for(dil):
        return jnp.where(ok, -slope[:, 0:1] * (stepsf * (float(dil) * LOG2E)), -jnp.inf)

    biases = [bias_for(d) for d in DILATIONS]
    blocks = []
    for p, dil in enumerate(DILATIONS):
        for n in range(seq // (dil * blk)):
            for r in range(dil):
                blocks.append((p, dil, r + n * dil * blk, n > 0))

    scored, probs = [], []
    for p, dil, base, with_prev in blocks:
        scored.append((scores(p, dil, base, biases[p], with_prev), p, dil, base, with_prev))
        if len(scored) > ATT_LOOKAHEAD:
            probs.append(softmax(*scored.pop(0)))
        if len(probs) > ATT_LOOKAHEAD:
            values(*probs.pop(0))
    for item in scored:
        probs.append(softmax(*item))
    for item in probs:
        values(*item)

    def merge_body(i, carry):
        rows = pl.ds(pl.multiple_of(i * rows_n, rows_n), rows_n)
        ms = [m_s[p, rows, :] for p in range(npat)]
        m = functools.reduce(jnp.maximum, ms)
        es = [jnp.exp2(mp - m) for mp in ms]
        den = functools.reduce(lambda a, b: a + b, [e * d_s[p, rows, :] for p, e in enumerate(es)])
        num = functools.reduce(lambda a, b: a + b, [e * o_s[p, rows, :] for p, e in enumerate(es)])
        o_ref[0, rows, :] = (num / den).astype(o_ref.dtype)
        return carry

    lax.fori_loop(0, seq // rows_n, merge_body, 0)


def _attention(proj, heads, col0):
    b, s, _ = proj.shape
    slopes = jnp.exp2(-8.0 * jnp.arange(1, heads + 1, dtype=F32) / heads)
    slopes = jnp.broadcast_to(slopes[:, None, None], (heads, 1, HEAD))

    def col(group):
        return pl.BlockSpec((1, s, HEAD), lambda i, j, group=group: (i, 0, col0 + group * heads + j))

    npat = len(DILATIONS)
    return pl.pallas_call(
        functools.partial(_attn_kernel, seq=s),
        grid=(b, heads),
        in_specs=[col(0), col(1), col(2),
                  pl.BlockSpec((1, 1, HEAD), lambda i, j: (j, 0, 0))],
        out_specs=pl.BlockSpec((1, s, HEAD), lambda i, j: (i, 0, j)),
        out_shape=jax.ShapeDtypeStruct((b, s, heads * HEAD), BF16),
        scratch_shapes=[pltpu.VMEM((npat, s, HEAD), F32), pltpu.VMEM((npat, s, HEAD), F32),
                        pltpu.VMEM((npat, s, HEAD), F32)],
        compiler_params=pltpu.CompilerParams(vmem_limit_bytes=VMEM_LIMIT),
        name="dilated_attn",
    )(proj, proj, proj, slopes)


def _outproj_norm_kernel(a_ref, b_ref, w_ref, x_ref, gate_ref, nw_ref, scale_ref, shift_ref, x1_ref, h_ref):
    ka = a_ref.shape[1]
    mix = _dot(a_ref[...], w_ref[0:ka, :]) + _dot(b_ref[...], w_ref[ka:, :])
    x1 = x_ref[...] + gate_ref[0] * mix
    x1_ref[...] = x1
    ms = jnp.mean(x1 * x1, axis=-1, keepdims=True)
    y = x1 * lax.rsqrt(ms + EPS) * nw_ref[...]
    h_ref[...] = (y * (1.0 + scale_ref[0]) + shift_ref[0]).astype(h_ref.dtype)


def _out_projection_norm(a, b, w, x, mod, gate_row, norm_w, scale_row, shift_row, seq, tm=512):
    m, ka = a.shape
    kb = b.shape[1]
    n = w.shape[1]
    per_seq = seq // tm

    def per_batch(row):
        return pl.BlockSpec((1, 1, n), lambda i: (row + i // per_seq, 0, 0))

    return pl.pallas_call(
        _outproj_norm_kernel,
        grid=(m // tm,),
        in_specs=[pl.BlockSpec((tm, ka), lambda i: (i, 0)),
                  pl.BlockSpec((tm, kb), lambda i: (i, 0)),
                  pl.BlockSpec((ka + kb, n), lambda i: (0, 0)),
                  pl.BlockSpec((tm, n), lambda i: (i, 0)),
                  per_batch(gate_row),
                  pl.BlockSpec((1, n), lambda i: (0, 0)),
                  per_batch(scale_row), per_batch(shift_row)],
        out_specs=[pl.BlockSpec((tm, n), lambda i: (i, 0)),
                   pl.BlockSpec((tm, n), lambda i: (i, 0))],
        out_shape=[jax.ShapeDtypeStruct((m, n), F32), jax.ShapeDtypeStruct((m, n), BF16)],
        compiler_params=pltpu.CompilerParams(vmem_limit_bytes=VMEM_LIMIT),
        name="out_proj_norm",
    )(a, b, w, x, mod, norm_w, mod, mod)


def _up_kernel(h_ref, wa_ref, wg_ref, cw_ref, cb_ref, wn_ref, y_ref, wn_out_ref, wa_s, wg_s, halo_s, *, per_seq):
    i = pl.program_id(1)
    wn_out_ref[...] = wn_ref[...].astype(BF16)

    @pl.when(i == 0)
    def _():
        wa_s[...] = wa_ref[...].astype(BF16)
        wg_s[...] = wg_ref[...].astype(BF16)

    @pl.when(i % per_seq == 0)
    def _():
        halo_s[...] = jnp.zeros_like(halo_s)

    h = h_ref[...]
    a = _dot(h, wa_s[...])
    g = _dot(h, wg_s[...])
    tm = a.shape[0]
    prev = halo_s[...]
    p1 = prev[7:8, :]
    p2 = prev[6:7, :]
    row = lax.broadcasted_iota(jnp.int32, a.shape, 0)
    a1 = jnp.where(row == 0, p1, pltpu.roll(a, 1, axis=0))
    a2 = jnp.where(row == 0, p2, jnp.where(row == 1, p1, pltpu.roll(a, 2, axis=0)))
    halo_s[...] = a[tm - 8:tm, :]
    cw = cw_ref[...]
    y = cb_ref[...] + a2 * cw[0:1, :] + a1 * cw[1:2, :] + a * cw[2:3, :]
    y_ref[...] = (_silu(y) * g).astype(y_ref.dtype)


def _up_projection(h, w, conv_w, conv_b, w_next, seq, tm=1024, tn=512):
    m, k = h.shape
    dff = w.shape[1] // 2
    nb = dff // tn
    mt = m // tm
    kn, nn = w_next.shape
    assert kn % (nb * mt * 16) == 0
    rows = kn // (nb * mt)
    return pl.pallas_call(
        functools.partial(_up_kernel, per_seq=seq // tm),
        grid=(nb, mt),
        in_specs=[pl.BlockSpec((tm, k), lambda j, i: (i, 0)),
                  pl.BlockSpec((k, tn), lambda j, i: (0, j)),
                  pl.BlockSpec((k, tn), lambda j, i: (0, nb + j)),
                  pl.BlockSpec((CONV_WIDTH, tn), lambda j, i: (0, j)),
                  pl.BlockSpec((1, tn), lambda j, i: (0, j)),
                  pl.BlockSpec((rows, nn), lambda j, i: (j * mt + i, 0))],
        out_specs=[pl.BlockSpec((tm, tn), lambda j, i: (i, j)),
                   pl.BlockSpec((rows, nn), lambda j, i: (j * mt + i, 0))],
        out_shape=[jax.ShapeDtypeStruct((m, dff), BF16), jax.ShapeDtypeStruct((kn, nn), BF16)],
        scratch_shapes=[pltpu.VMEM((k, tn), BF16), pltpu.VMEM((k, tn), BF16), pltpu.VMEM((8, tn), F32)],
        compiler_params=pltpu.CompilerParams(
            dimension_semantics=("arbitrary", "arbitrary"), vmem_limit_bytes=VMEM_LIMIT),
        name="up_proj_conv_gate",
    )(h, w, w, conv_w, conv_b, w_next)


def _down_kernel(y_ref, w_ref, x_ref, gate_ref, o_ref):
    o_ref[...] = x_ref[...] + gate_ref[0] * _dot(y_ref[...], w_ref[...])


def _down_projection(y, w, x, mod, gate_row, seq, tm=512, tn=1024):
    m, k = y.shape
    n = w.shape[1]
    per_seq = seq // tm
    return pl.pallas_call(
        _down_kernel,
        grid=(n // tn, m // tm),
        in_specs=[pl.BlockSpec((tm, k), lambda j, i: (i, 0)),
                  pl.BlockSpec((k, tn), lambda j, i: (0, j)),
                  pl.BlockSpec((tm, tn), lambda j, i: (i, j)),
                  pl.BlockSpec((1, 1, tn), lambda j, i: (gate_row + i // per_seq, 0, j))],
        out_specs=pl.BlockSpec((tm, tn), lambda j, i: (i, j)),
        out_shape=jax.ShapeDtypeStruct((m, n), F32),
        compiler_params=pltpu.CompilerParams(
            dimension_semantics=("arbitrary", "arbitrary"), vmem_limit_bytes=VMEM_LIMIT),
        name="down_proj",
    )(y, w, x, mod)


def _layer(layer, x, mod, norm1_w, w_in, lb_logits, hg_norm_w, q_norm_w, k_norm_w, w_out, norm2_w, w_up, conv_w,
           conv_b, w_down):
    b, s, d = x.shape
    shift1, scale1, gate1, shift2, scale2, gate2 = [j * (mod.shape[0] // 6) for j in range(6)]

    h = _norm_modulate(x, norm1_w.reshape(1, d), mod, scale1, shift1)
    tn = ATT_HEADS * HEAD
    q_tile = 4 * HG_HEADS * HEAD // tn
    n_tiles = w_in.shape[1] // tn
    head_w = jnp.ones((n_tiles, 1, tn), F32)
    head_w = head_w.at[q_tile, 0].set(jnp.tile(q_norm_w * (HEAD ** -0.5 * LOG2E), ATT_HEADS))
    head_w = head_w.at[q_tile + 1, 0].set(jnp.tile(k_norm_w, ATT_HEADS))
    proj = _in_projection(h.reshape(b * s, d), w_in, head_w, (q_tile, q_tile + 1), tn=tn).reshape(b, s, -1)
    a_out, w_out_bf16 = _hgrn(proj, lb_logits, hg_norm_w.reshape(1, HEAD), w_out, HG_HEADS, layer)
    b_out = _attention(proj, ATT_HEADS, 4 * HG_HEADS)
    x1, h2 = _out_projection_norm(a_out.reshape(b * s, -1), b_out.reshape(b * s, -1), w_out_bf16,
                                  x.reshape(b * s, d), mod, gate1, norm2_w.reshape(1, d), scale2, shift2, s)
    y, w_down_bf16 = _up_projection(h2, w_up, conv_w, conv_b.reshape(1, -1), w_down, s)
    x2 = _down_projection(y, w_down_bf16, x1, mod, gate2, s)
    return x2.reshape(b, s, d)


def kernel(x, c, w_ada, b_ada, norm1_w, w_in, lb_logits, hg_norm_w, q_norm_w, k_norm_w, w_out, norm2_w, w_up, conv_w, conv_b, w_down):
    depth = w_ada.shape[0]
    b = x.shape[0]
    c_pad = jnp.pad(c, ((0, 8 - b), (0, 0)))
    for l in range(depth):
        mod = _modulation(c_pad, w_ada[l], b_ada[l].reshape(1, -1))
        x = _layer(l, x, mod, norm1_w[l], w_in[l], lb_logits, hg_norm_w[l], q_norm_w[l], k_norm_w[l],
                   w_out[l], norm2_w[l], w_up[l], conv_w[l], conv_b[l], w_down[l])
    return x
```

```python
import functools

import numpy as np
import jax
import jax.numpy as jnp
from jax import lax
from jax.experimental import pallas as pl
from jax.experimental.pallas import tpu as pltpu

F32 = jnp.float32
BF16 = jnp.bfloat16
EPS = 1e-6
LOG2E = 1.4426950408889634

HEAD = 128
SUB = 8
HG_HEADS = 8
ATT_HEADS = 8
N_BACK = 128
DILATIONS = (1, 4, 16)
CONV_WIDTH = 3
HG_CHUNK = 64
HG_UNROLL = 8
HG_LOOKAHEAD = 5
ATT_LOOKAHEAD = 1
VMEM_LIMIT = 56 * 1024 * 1024


def _dot(a, b):
    return jnp.dot(a, b, preferred_element_type=F32)


def _dot_nt(a, b):
    return lax.dot_general(a, b, (((1,), (1,)), ((), ())), preferred_element_type=F32)


def _dot_tn(a, b):
    return lax.dot_general(a, b, (((0,), (0,)), ((), ())), preferred_element_type=F32)


def _silu(v):
    return v * jax.nn.sigmoid(v)


def _mod_kernel(c_ref, w_ref, b_ref, o_ref):
    ca = _silu(c_ref[...]).astype(BF16)
    o_ref[0] = _dot(ca, w_ref[...].astype(BF16)) + b_ref[...]


def _modulation(c_pad, w_ada, b_ada, tn=1024):
    rows, d = c_pad.shape
    n = w_ada.shape[1]
    per = d // tn
    out = pl.pallas_call(
        _mod_kernel,
        grid=(n // tn,),
        in_specs=[pl.BlockSpec((rows, d), lambda j: (0, 0)),
                  pl.BlockSpec((d, tn), lambda j: (0, j)),
                  pl.BlockSpec((1, tn), lambda j: (0, j))],
        out_specs=pl.BlockSpec((1, rows, tn), lambda j: (j // per, 0, j % per)),
        out_shape=jax.ShapeDtypeStruct((n // d, rows, d), F32),
        compiler_params=pltpu.CompilerParams(vmem_limit_bytes=VMEM_LIMIT),
        name="adaln_mod",
    )(c_pad, w_ada, b_ada)
    return out.reshape(n // d * rows, 1, d)


def _norm_kernel(x_ref, w_ref, scale_ref, shift_ref, o_ref):
    x = x_ref[0]
    ms = jnp.mean(x * x, axis=-1, keepdims=True)
    y = x * lax.rsqrt(ms + EPS) * w_ref[...]
    o_ref[0] = (y * (1.0 + scale_ref[0]) + shift_ref[0]).astype(o_ref.dtype)


def _norm_modulate(x, w, mod, scale_row, shift_row, ts=1024):
    b, s, d = x.shape
    return pl.pallas_call(
        _norm_kernel,
        grid=(b, s // ts),
        in_specs=[pl.BlockSpec((1, ts, d), lambda i, j: (i, j, 0)),
                  pl.BlockSpec((1, d), lambda i, j: (0, 0)),
                  pl.BlockSpec((1, 1, d), lambda i, j: (scale_row + i, 0, 0)),
                  pl.BlockSpec((1, 1, d), lambda i, j: (shift_row + i, 0, 0))],
        out_specs=pl.BlockSpec((1, ts, d), lambda i, j: (i, j, 0)),
        out_shape=jax.ShapeDtypeStruct((b, s, d), BF16),
        compiler_params=pltpu.CompilerParams(vmem_limit_bytes=VMEM_LIMIT),
        name="norm_modulate",
    )(x, w, mod, mod)


def _inproj_kernel(x_ref, w_ref, nw_ref, o_ref, wb_ref, *, norm_tiles):
    j = pl.program_id(0)

    @pl.when(pl.program_id(1) == 0)
    def _():
        wb_ref[...] = w_ref[...].astype(BF16)

    normed = functools.reduce(jnp.logical_or, [j == t for t in norm_tiles])

    @pl.when(jnp.logical_not(normed))
    def _():
        o_ref[...] = _dot(x_ref[...], wb_ref[...])

    @pl.when(normed)
    def _():
        nw = nw_ref[0]
        half = o_ref.shape[1] // 2
        accs = [_dot(x_ref[...], wb_ref[:, c0:c0 + half]) for c0 in (0, half)]
        for c0, acc in zip((0, half), accs):
            for c1 in range(0, half, HEAD):
                seg = acc[:, c1:c1 + HEAD]
                ms = jnp.mean(seg * seg, axis=-1, keepdims=True)
                o_ref[:, c0 + c1:c0 + c1 + HEAD] = seg * lax.rsqrt(ms + EPS) * nw[:, c0 + c1:c0 + c1 + HEAD]


def _in_projection(h, w, norm_w, norm_tiles, tm=1024, tn=1024):
    m, k = h.shape
    n = w.shape[1]
    return pl.pallas_call(
        functools.partial(_inproj_kernel, norm_tiles=norm_tiles),
        grid=(n // tn, m // tm),
        in_specs=[pl.BlockSpec((tm, k), lambda j, i: (i, 0)),
                  pl.BlockSpec((k, tn), lambda j, i: (0, j)),
                  pl.BlockSpec((1, 1, tn), lambda j, i: (j, 0, 0))],
        out_specs=pl.BlockSpec((tm, tn), lambda j, i: (i, j)),
        out_shape=jax.ShapeDtypeStruct((m, n), F32),
        scratch_shapes=[pltpu.VMEM((k, tn), BF16)],
        compiler_params=pltpu.CompilerParams(
            dimension_semantics=("arbitrary", "arbitrary"), vmem_limit_bytes=VMEM_LIMIT),
        name="in_proj",
    )(h, w, norm_w)


def _hgrn_tables(c):
    t = np.arange(c)[:, None]
    u = np.arange(c)[None, :]
    masks = []
    m = c
    while m >= 2:
        half = m // 2
        start = (t // m) * m
        u_start = (u // m) * m
        masks.append(((t - start) >= half) & ((u - u_start) < half) & (u_start == start))
        m //= 2
    cum = (u <= t).astype(np.float32)
    return np.concatenate([cum, cum, cum], axis=1), np.stack(masks).astype(np.float32)


def _level_signs(c, width):
    t = lax.broadcasted_iota(jnp.int32, (c, width), 0)
    signs = []
    m = c
    while m >= 8:
        signs.append(jnp.where(t % m >= m // 2, 1.0, -1.0).astype(F32))
        m //= 2
    return signs


def _level_log_decays(g, cum, signs):
    c = g.shape[0]
    out = []
    m = c
    for sgn in signs:
        half = m // 2
        mids = [jnp.broadcast_to(cum[j * m + half - 1:j * m + half, :], (m, g.shape[1])) for j in range(c // m)]
        out.append((cum - jnp.concatenate(mids, axis=0)) * sgn)
        m //= 2
    pos = lax.broadcasted_iota(jnp.int32, g.shape, 0) % 4
    g_next = pltpu.roll(g, c - 1, axis=0)
    g_prev = pltpu.roll(g, 1, axis=0)
    out.append(jnp.where(pos == 0, g_next, jnp.where(pos == 1, 0.0, jnp.where(pos == 2, g, g + g_prev))))
    return out


def _hgrn_kernel(hq_ref, hf_ref, hi_ref, hg_ref, lbl_ref, nw_ref, tab_ref, msk_ref, wn_ref, o_ref, wn_out_ref,
                 st_ref, *, chunk, heads, layer):
    c = chunk
    assert chunk >= 8 and 2 ** msk_ref.shape[0] == chunk
    ts = hq_ref.shape[1]
    wn_out_ref[...] = wn_ref[...].astype(BF16)

    @pl.when(pl.program_id(1) == 0)
    def _():
        st_ref[...] = jnp.zeros_like(st_ref)

    lbl = lbl_ref[...]
    ex = jnp.exp(lbl - jnp.max(lbl, axis=0, keepdims=True))
    lb_all = jnp.sum(ex[0:layer + 1], axis=0, keepdims=True) / jnp.sum(ex, axis=0, keepdims=True)
    nw = nw_ref[...]
    t_idx = lax.broadcasted_iota(jnp.int32, (c, c), 0)
    s_idx = lax.broadcasted_iota(jnp.int32, (c, c), 1)
    near_diag = (s_idx == t_idx) | ((t_idx % 2 == 1) & (s_idx == t_idx - 1))
    odd = lax.broadcasted_iota(jnp.int32, (c, HEAD), 0) % 2 == 1
    signs = _level_signs(c, HEAD)

    def gates(rows, h):
        cols = slice(h * HEAD, (h + 1) * HEAD)
        lb = lb_all[:, cols]
        q = _silu(hq_ref[0, rows, cols])
        f = lb + (1.0 - lb) * jax.nn.sigmoid(hf_ref[0, rows, cols])
        g = jnp.log(f) * LOG2E
        g1 = g.astype(BF16)
        r1 = g - g1.astype(F32)
        g2 = r1.astype(BF16)
        g3 = (r1 - g2.astype(F32)).astype(BF16)
        cum = _dot(tab_ref[...], jnp.concatenate([g1, g2, g3], axis=0))
        return rows, h, q, f, g, cum

    def pair_products(rows, h, q, f, g, cum):
        cols = slice(h * HEAD, (h + 1) * HEAD)
        kk = 1.0 - f
        v = hi_ref[0, rows, cols].astype(BF16)
        e_cum = jnp.exp2(cum)
        e_last = jnp.exp2(cum[c - 1:c, :] - cum)

        state_t = st_ref[h]
        o = _dot_nt((q * e_cum).astype(BF16), state_t.astype(BF16))
        k_dec = (kk * e_last).astype(BF16)
        st_ref[h] = state_t * e_cum[c - 1:c, :] + _dot_tn(v, k_dec)

        e2 = jnp.where(odd, f, 1.0)
        lhs = jnp.concatenate([q * e2, jnp.where(odd, q * (1.0 - f * f), 0.0)], axis=1).astype(BF16)
        rhs = jnp.concatenate([kk * e2, jnp.where(odd, kk, 0.0)], axis=1).astype(BF16)
        a = jnp.where(near_diag, _dot_nt(lhs, rhs), 0.0)
        nb = c // SUB
        blocks = [a[i * SUB:(i + 1) * SUB] for i in range(nb)]
        m = c
        for l, d in enumerate(_level_log_decays(g, cum, signs)):
            e = jnp.exp2(d)
            if m >= 2 * SUB:
                ups = [i for i in range(nb) if (i * SUB) % m >= m // 2]
                tile = lambda x, i: x[i * SUB:(i + 1) * SUB]
                lhs = jnp.concatenate([tile(q, i) * tile(e, i) for i in ups], axis=0)
                rhs = jnp.concatenate([tile(kk, i) if i in ups else tile(kk, i) * tile(e, i) for i in range(nb)],
                                      axis=0)
            else:
                ups = list(range(nb))
                lhs, rhs = q * e, kk * e
            sc = _dot_nt(lhs.astype(BF16), rhs.astype(BF16))
            for j, i in enumerate(ups):
                keep = msk_ref[l, i * SUB:(i + 1) * SUB, :] > 0.5
                blocks[i] = jnp.where(keep, sc[j * SUB:(j + 1) * SUB], blocks[i])
            m //= 2
        return rows, h, o, jnp.concatenate(blocks, axis=0), v

    def output(rows, h, o, a, v):
        cols = slice(h * HEAD, (h + 1) * HEAD)
        o = o + _dot(a.astype(BF16), v)
        ms = jnp.mean(o * o, axis=-1, keepdims=True)
        on = o * lax.rsqrt(ms + EPS) * nw
        o_ref[0, rows, cols] = (on * _silu(hg_ref[0, rows, cols])).astype(o_ref.dtype)

    def body(i, carry):
        after_gates, after_pairs = [], []
        for u in range(HG_UNROLL):
            rows = pl.ds(pl.multiple_of((i * HG_UNROLL + u) * c, c), c)
            for h in range(heads):
                after_gates.append(gates(rows, h))
                if len(after_gates) > HG_LOOKAHEAD:
                    after_pairs.append(pair_products(*after_gates.pop(0)))
                if len(after_pairs) > HG_LOOKAHEAD:
                    output(*after_pairs.pop(0))
        for item in after_gates:
            after_pairs.append(pair_products(*item))
        for item in after_pairs:
            output(*item)
        return carry

    lax.fori_loop(0, ts // (c * HG_UNROLL), body, 0)


def _hgrn(proj, lb_logits, norm_w, w_next, heads, layer, ts=512):
    b, s, _ = proj.shape
    nl = lb_logits.shape[0]
    w = heads * HEAD
    st = s // ts
    kn, nn = w_next.shape
    assert kn % (b * st * 16) == 0
    rows = kn // (b * st)
    tab3, msk = _hgrn_tables(HG_CHUNK)
    tab3 = jnp.asarray(tab3, BF16)
    msk = jnp.asarray(msk, F32)

    def col(group):
        return pl.BlockSpec((1, ts, w), lambda i, j, group=group: (i, j, group))

    return pl.pallas_call(
        functools.partial(_hgrn_kernel, chunk=HG_CHUNK, heads=heads, layer=layer),
        grid=(b, st),
        in_specs=[col(0), col(1), col(2), col(3),
                  pl.BlockSpec((nl, w), lambda i, j: (0, 0)),
                  pl.BlockSpec((1, HEAD), lambda i, j: (0, 0)),
                  pl.BlockSpec(tab3.shape, lambda i, j: (0, 0)),
                  pl.BlockSpec(msk.shape, lambda i, j: (0, 0, 0)),
                  pl.BlockSpec((rows, nn), lambda i, j: (i * st + j, 0))],
        out_specs=[pl.BlockSpec((1, ts, w), lambda i, j: (i, j, 0)),
                   pl.BlockSpec((rows, nn), lambda i, j: (i * st + j, 0))],
        out_shape=[jax.ShapeDtypeStruct((b, s, w), BF16), jax.ShapeDtypeStruct((kn, nn), BF16)],
        scratch_shapes=[pltpu.VMEM((heads, HEAD, HEAD), F32)],
        compiler_params=pltpu.CompilerParams(
            dimension_semantics=("arbitrary", "arbitrary"), vmem_limit_bytes=VMEM_LIMIT),
        name="hgrn2",
    )(proj, proj, proj, proj, lb_logits, norm_w, tab3, msk, w_next)


def _attn_kernel(q_ref, k_ref, v_ref, slope_ref, o_ref, o_s, m_s, d_s, *, seq):
    assert DILATIONS == (1, 4, 16) and seq == 16 * N_BACK
    blk = N_BACK
    npat = len(DILATIONS)
    slope = slope_ref[0]
    rows_n = 256

    qi = lax.broadcasted_iota(jnp.int32, (blk, 2 * blk), 0)
    kj = lax.broadcasted_iota(jnp.int32, (blk, 2 * blk), 1)
    steps = qi + blk - kj
    ok = (steps >= 0) & (steps <= N_BACK)
    stepsf = steps.astype(F32)

    def scores(p, dil, base, bias, with_prev):
        qb = q_ref[0, pl.ds(base, blk, stride=dil), :].astype(BF16)
        if with_prev:
            kb = k_ref[0, pl.ds(base - blk * dil, 2 * blk, stride=dil), :].astype(BF16)
            sc = _dot_nt(qb, kb) + bias
        else:
            kb = k_ref[0, pl.ds(base, blk, stride=dil), :].astype(BF16)
            sc = _dot_nt(qb, kb) + bias[:, blk:]
        return sc

    def softmax(sc, p, dil, base, with_prev):
        m = jnp.max(sc, axis=-1, keepdims=True)
        pr = jnp.exp2(sc - m).astype(BF16)
        m_s[p, pl.ds(base, blk, stride=dil), :] = jnp.broadcast_to(m, (blk, HEAD))
        return pr, p, dil, base, with_prev

    def values(pr, p, dil, base, with_prev):
        if with_prev:
            vb = v_ref[0, pl.ds(base - blk * dil, 2 * blk, stride=dil), :].astype(BF16)
        else:
            vb = v_ref[0, pl.ds(base, blk, stride=dil), :].astype(BF16)
        res = _dot(pr, jnp.concatenate([vb, jnp.ones_like(vb)], axis=1))
        o_s[p, pl.ds(base, blk, stride=dil), :] = res[:, :HEAD]
        d_s[p, pl.ds(base, blk, stride=dil), :] = res[:, HEAD:]

    def bias_for(dil):
        return jnp.where(ok, -slope[:, 0:1] * (stepsf * (float(dil) * LOG2E)), -jnp.inf)

    biases = [bias_for(d) for d in DILATIONS]
    blocks = []
    for p, dil in enumerate(DILATIONS):
        for n in range(seq // (dil * blk)):
            for r in range(dil):
                blocks.append((p, dil, r + n * dil * blk, n > 0))

    scored, probs = [], []
    for p, dil, base, with_prev in blocks:
        scored.append((scores(p, dil, base, biases[p], with_prev), p, dil, base, with_prev))
        if len(scored) > ATT_LOOKAHEAD:
            probs.append(softmax(*scored.pop(0)))
        if len(probs) > ATT_LOOKAHEAD:
            values(*probs.pop(0))
    for item in scored:
        probs.append(softmax(*item))
    for item in probs:
        values(*item)

    def merge_body(i, carry):
        rows = pl.ds(pl.multiple_of(i * rows_n, rows_n), rows_n)
        ms = [m_s[p, rows, :] for p in range(npat)]
        m = functools.reduce(jnp.maximum, ms)
        es = [jnp.exp2(mp - m) for mp in ms]
        den = functools.reduce(lambda a, b: a + b, [e * d_s[p, rows, :] for p, e in enumerate(es)])
        num = functools.reduce(lambda a, b: a + b, [e * o_s[p, rows, :] for p, e in enumerate(es)])
        o_ref[0, rows, :] = (num / den).astype(o_ref.dtype)
        return carry

    lax.fori_loop(0, seq // rows_n, merge_body, 0)


def _attention(proj, heads, col0):
    b, s, _ = proj.shape
    slopes = jnp.exp2(-8.0 * jnp.arange(1, heads + 1, dtype=F32) / heads)
    slopes = jnp.broadcast_to(slopes[:, None, None], (heads, 1, HEAD))

    def col(group):
        return pl.BlockSpec((1, s, HEAD), lambda i, j, group=group: (i, 0, col0 + group * heads + j))

    npat = len(DILATIONS)
    return pl.pallas_call(
        functools.partial(_attn_kernel, seq=s),
        grid=(b, heads),
        in_specs=[col(0), col(1), col(2),
                  pl.BlockSpec((1, 1, HEAD), lambda i, j: (j, 0, 0))],
        out_specs=pl.BlockSpec((1, s, HEAD), lambda i, j: (i, 0, j)),
        out_shape=jax.ShapeDtypeStruct((b, s, heads * HEAD), BF16),
        scratch_shapes=[pltpu.VMEM((npat, s, HEAD), F32), pltpu.VMEM((npat, s, HEAD), F32),
                        pltpu.VMEM((npat, s, HEAD), F32)],
        compiler_params=pltpu.CompilerParams(vmem_limit_bytes=VMEM_LIMIT),
        name="dilated_attn",
    )(proj, proj, proj, slopes)


def _outproj_norm_kernel(a_ref, b_ref, w_ref, x_ref, gate_ref, nw_ref, scale_ref, shift_ref, x1_ref, h_ref):
    ka = a_ref.shape[1]
    mix = _dot(a_ref[...], w_ref[0:ka, :]) + _dot(b_ref[...], w_ref[ka:, :])
    x1 = x_ref[...] + gate_ref[0] * mix
    x1_ref[...] = x1
    ms = jnp.mean(x1 * x1, axis=-1, keepdims=True)
    y = x1 * lax.rsqrt(ms + EPS) * nw_ref[...]
    h_ref[...] = (y * (1.0 + scale_ref[0]) + shift_ref[0]).astype(h_ref.dtype)


def _out_projection_norm(a, b, w, x, mod, gate_row, norm_w, scale_row, shift_row, seq, tm=512):
    m, ka = a.shape
    kb = b.shape[1]
    n = w.shape[1]
    per_seq = seq // tm

    def per_batch(row):
        return pl.BlockSpec((1, 1, n), lambda i: (row + i // per_seq, 0, 0))

    return pl.pallas_call(
        _outproj_norm_kernel,
        grid=(m // tm,),
        in_specs=[pl.BlockSpec((tm, ka), lambda i: (i, 0)),
                  pl.BlockSpec((tm, kb), lambda i: (i, 0)),
                  pl.BlockSpec((ka + kb, n), lambda i: (0, 0)),
                  pl.BlockSpec((tm, n), lambda i: (i, 0)),
                  per_batch(gate_row),
                  pl.BlockSpec((1, n), lambda i: (0, 0)),
                  per_batch(scale_row), per_batch(shift_row)],
        out_specs=[pl.BlockSpec((tm, n), lambda i: (i, 0)),
                   pl.BlockSpec((tm, n), lambda i: (i, 0))],
        out_shape=[jax.ShapeDtypeStruct((m, n), F32), jax.ShapeDtypeStruct((m, n), BF16)],
        compiler_params=pltpu.CompilerParams(vmem_limit_bytes=VMEM_LIMIT),
        name="out_proj_norm",
    )(a, b, w, x, mod, norm_w, mod, mod)


def _up_kernel(h_ref, wa_ref, wg_ref, cw_ref, cb_ref, wn_ref, y_ref, wn_out_ref, wa_s, wg_s, halo_s, *, per_seq):
    i = pl.program_id(1)
    wn_out_ref[...] = wn_ref[...].astype(BF16)

    @pl.when(i == 0)
    def _():
        wa_s[...] = wa_ref[...].astype(BF16)
        wg_s[...] = wg_ref[...].astype(BF16)

    @pl.when(i % per_seq == 0)
    def _():
        halo_s[...] = jnp.zeros_like(halo_s)

    h = h_ref[...]
    a = _dot(h, wa_s[...])
    g = _dot(h, wg_s[...])
    tm = a.shape[0]
    prev = halo_s[...]
    p1 = prev[7:8, :]
    p2 = prev[6:7, :]
    row = lax.broadcasted_iota(jnp.int32, a.shape, 0)
    a1 = jnp.where(row == 0, p1, pltpu.roll(a, 1, axis=0))
    a2 = jnp.where(row == 0, p2, jnp.where(row == 1, p1, pltpu.roll(a, 2, axis=0)))
    halo_s[...] = a[tm - 8:tm, :]
    cw = cw_ref[...]
    y = cb_ref[...] + a2 * cw[0:1, :] + a1 * cw[1:2, :] + a * cw[2:3, :]
    y_ref[...] = (_silu(y) * g).astype(y_ref.dtype)


def _up_projection(h, w, conv_w, conv_b, w_next, seq, tm=1024, tn=512):
    m, k = h.shape
    dff = w.shape[1] // 2
    nb = dff // tn
    mt = m // tm
    kn, nn = w_next.shape
    assert kn % (nb * mt * 16) == 0
    rows = kn // (nb * mt)
    return pl.pallas_call(
        functools.partial(_up_kernel, per_seq=seq // tm),
        grid=(nb, mt),
        in_specs=[pl.BlockSpec((tm, k), lambda j, i: (i, 0)),
                  pl.BlockSpec((k, tn), lambda j, i: (0, j)),
                  pl.BlockSpec((k, tn), lambda j, i: (0, nb + j)),
                  pl.BlockSpec((CONV_WIDTH, tn), lambda j, i: (0, j)),
                  pl.BlockSpec((1, tn), lambda j, i: (0, j)),
                  pl.BlockSpec((rows, nn), lambda j, i: (j * mt + i, 0))],
        out_specs=[pl.BlockSpec((tm, tn), lambda j, i: (i, j)),
                   pl.BlockSpec((rows, nn), lambda j, i: (j * mt + i, 0))],
        out_shape=[jax.ShapeDtypeStruct((m, dff), BF16), jax.ShapeDtypeStruct((kn, nn), BF16)],
        scratch_shapes=[pltpu.VMEM((k, tn), BF16), pltpu.VMEM((k, tn), BF16), pltpu.VMEM((8, tn), F32)],
        compiler_params=pltpu.CompilerParams(
            dimension_semantics=("arbitrary", "arbitrary"), vmem_limit_bytes=VMEM_LIMIT),
        name="up_proj_conv_gate",
    )(h, w, w, conv_w, conv_b, w_next)


def _down_kernel(y_ref, w_ref, x_ref, gate_ref, o_ref):
    o_ref[...] = x_ref[...] + gate_ref[0] * _dot(y_ref[...], w_ref[...])


def _down_projection(y, w, x, mod, gate_row, seq, tm=512, tn=1024):
    m, k = y.shape
    n = w.shape[1]
    per_seq = seq // tm
    return pl.pallas_call(
        _down_kernel,
        grid=(n // tn, m // tm),
        in_specs=[pl.BlockSpec((tm, k), lambda j, i: (i, 0)),
                  pl.BlockSpec((k, tn), lambda j, i: (0, j)),
                  pl.BlockSpec((tm, tn), lambda j, i: (i, j)),
                  pl.BlockSpec((1, 1, tn), lambda j, i: (gate_row + i // per_seq, 0, j))],
        out_specs=pl.BlockSpec((tm, tn), lambda j, i: (i, j)),
        out_shape=jax.ShapeDtypeStruct((m, n), F32),
        compiler_params=pltpu.CompilerParams(
            dimension_semantics=("arbitrary", "arbitrary"), vmem_limit_bytes=VMEM_LIMIT),
        name="down_proj",
    )(y, w, x, mod)


def _layer(layer, x, mod, norm1_w, w_in, lb_logits, hg_norm_w, q_norm_w, k_norm_w, w_out, norm2_w, w_up, conv_w,
           conv_b, w_down):
    b, s, d = x.shape
    shift1, scale1, gate1, shift2, scale2, gate2 = [j * (mod.shape[0] // 6) for j in range(6)]

    h = _norm_modulate(x, norm1_w.reshape(1, d), mod, scale1, shift1)
    tn = ATT_HEADS * HEAD
    q_tile = 4 * HG_HEADS * HEAD // tn
    n_tiles = w_in.shape[1] // tn
    head_w = jnp.ones((n_tiles, 1, tn), F32)
    head_w = head_w.at[q_tile, 0].set(jnp.tile(q_norm_w * (HEAD ** -0.5 * LOG2E), ATT_HEADS))
    head_w = head_w.at[q_tile + 1, 0].set(jnp.tile(k_norm_w, ATT_HEADS))
    proj = _in_projection(h.reshape(b * s, d), w_in, head_w, (q_tile, q_tile + 1), tn=tn).reshape(b, s, -1)
    a_out, w_out_bf16 = _hgrn(proj, lb_logits, hg_norm_w.reshape(1, HEAD), w_out, HG_HEADS, layer)
    b_out = _attention(proj, ATT_HEADS, 4 * HG_HEADS)
    x1, h2 = _out_projection_norm(a_out.reshape(b * s, -1), b_out.reshape(b * s, -1), w_out_bf16,
                                  x.reshape(b * s, d), mod, gate1, norm2_w.reshape(1, d), scale2, shift2, s)
    y, w_down_bf16 = _up_projection(h2, w_up, conv_w, conv_b.reshape(1, -1), w_down, s)
    x2 = _down_projection(y, w_down_bf16, x1, mod, gate2, s)
    return x2.reshape(b, s, d)


def kernel(x, c, w_ada, b_ada, norm1_w, w_in, lb_logits, hg_norm_w, q_norm_w, k_norm_w, w_out, norm2_w, w_up, conv_w, conv_b, w_down):
    depth = w_ada.shape[0]
    b = x.shape[0]
    c_pad = jnp.pad(c, ((0, 8 - b), (0, 0)))
    for l in range(depth):
        mod = _modulation(c_pad, w_ada[l], b_ada[l].reshape(1, -1))
        x = _layer(l, x, mod, norm1_w[l], w_in[l], lb_logits, hg_norm_w[l], q_norm_w[l], k_norm_w[l],
                   w_out[l], norm2_w[l], w_up[l], conv_w[l], conv_b[l], w_down[l])
    return x
```

```python
import functools

import numpy as np
import jax
import jax.numpy as jnp
from jax import lax
from jax.experimental import pallas as pl
from jax.experimental.pallas import tpu as pltpu

F32 = jnp.float32
BF16 = jnp.bfloat16
EPS = 1e-6
LOG2E = 1.4426950408889634

HEAD = 128
SUB = 8
HG_HEADS = 8
ATT_HEADS = 8
N_BACK = 128
DILATIONS = (1, 4, 16)
CONV_WIDTH = 3
HG_CHUNK = 64
HG_UNROLL = 8
HG_LOOKAHEAD = 5
ATT_LOOKAHEAD = 1
VMEM_LIMIT = 56 * 1024 * 1024


def _dot(a, b):
    return jnp.dot(a, b, preferred_element_type=F32)


def _dot_nt(a, b):
    return lax.dot_general(a, b, (((1,), (1,)), ((), ())), preferred_element_type=F32)


def _dot_tn(a, b):
    return lax.dot_general(a, b, (((0,), (0,)), ((), ())), preferred_element_type=F32)


def _silu(v):
    return v * jax.nn.sigmoid(v)


def _mod_kernel(c_ref, w_ref, b_ref, o_ref):
    ca = _silu(c_ref[...]).astype(BF16)
    o_ref[0] = _dot(ca, w_ref[...].astype(BF16)) + b_ref[...]


def _modulation(c_pad, w_ada, b_ada, chunks, tn=1024):
    rows, d = c_pad.shape
    n = chunks * d
    per = d // tn
    out = pl.pallas_call(
        _mod_kernel,
        grid=(n // tn,),
        in_specs=[pl.BlockSpec((rows, d), lambda j: (0, 0)),
                  pl.BlockSpec((d, tn), lambda j: (0, j)),
                  pl.BlockSpec((1, tn), lambda j: (0, j))],
        out_specs=pl.BlockSpec((1, rows, tn), lambda j: (j // per, 0, j % per)),
        out_shape=jax.ShapeDtypeStruct((n // d, rows, d), F32),
        compiler_params=pltpu.CompilerParams(vmem_limit_bytes=VMEM_LIMIT),
        name="adaln_mod",
    )(c_pad, w_ada, b_ada)
    return out.reshape(n // d * rows, 1, d)


def _norm_kernel(x_ref, w_ref, scale_ref, shift_ref, o_ref):
    x = x_ref[0]
    ms = jnp.mean(x * x, axis=-1, keepdims=True)
    y = x * lax.rsqrt(ms + EPS) * w_ref[...]
    o_ref[0] = (y * (1.0 + scale_ref[0]) + shift_ref[0]).astype(o_ref.dtype)


def _norm_modulate(x, w, mod, scale_row, shift_row, ts=1024):
    b, s, d = x.shape
    return pl.pallas_call(
        _norm_kernel,
        grid=(b, s // ts),
        in_specs=[pl.BlockSpec((1, ts, d), lambda i, j: (i, j, 0)),
                  pl.BlockSpec((1, d), lambda i, j: (0, 0)),
                  pl.BlockSpec((1, 1, d), lambda i, j: (scale_row + i, 0, 0)),
                  pl.BlockSpec((1, 1, d), lambda i, j: (shift_row + i, 0, 0))],
        out_specs=pl.BlockSpec((1, ts, d), lambda i, j: (i, j, 0)),
        out_shape=jax.ShapeDtypeStruct((b, s, d), BF16),
        compiler_params=pltpu.CompilerParams(vmem_limit_bytes=VMEM_LIMIT),
        name="norm_modulate",
    )(x, w, mod, mod)


def _inproj_kernel(x_ref, w_ref, nw_ref, o_ref, wb_ref, *, norm_tiles):
    j = pl.program_id(0)

    @pl.when(pl.program_id(1) == 0)
    def _():
        wb_ref[...] = w_ref[...].astype(BF16)

    normed = functools.reduce(jnp.logical_or, [j == t for t in norm_tiles])

    @pl.when(jnp.logical_not(normed))
    def _():
        o_ref[...] = _dot(x_ref[...], wb_ref[...])

    @pl.when(normed)
    def _():
        nw = nw_ref[0]
        half = o_ref.shape[1] // 2
        accs = [_dot(x_ref[...], wb_ref[:, c0:c0 + half]) for c0 in (0, half)]
        for c0, acc in zip((0, half), accs):
            for c1 in range(0, half, HEAD):
                seg = acc[:, c1:c1 + HEAD]
                ms = jnp.mean(seg * seg, axis=-1, keepdims=True)
                o_ref[:, c0 + c1:c0 + c1 + HEAD] = seg * lax.rsqrt(ms + EPS) * nw[:, c0 + c1:c0 + c1 + HEAD]


def _in_projection(h, w, norm_w, norm_tiles, tm=1024, tn=1024):
    m, k = h.shape
    n = w.shape[1]
    return pl.pallas_call(
        functools.partial(_inproj_kernel, norm_tiles=norm_tiles),
        grid=(n // tn, m // tm),
        in_specs=[pl.BlockSpec((tm, k), lambda j, i: (i, 0)),
                  pl.BlockSpec((k, tn), lambda j, i: (0, j)),
                  pl.BlockSpec((1, 1, tn), lambda j, i: (j, 0, 0))],
        out_specs=pl.BlockSpec((tm, tn), lambda j, i: (i, j)),
        out_shape=jax.ShapeDtypeStruct((m, n), F32),
        scratch_shapes=[pltpu.VMEM((k, tn), BF16)],
        compiler_params=pltpu.CompilerParams(
            dimension_semantics=("arbitrary", "arbitrary"), vmem_limit_bytes=VMEM_LIMIT),
        name="in_proj",
    )(h, w, norm_w)


def _hgrn_tables(c):
    t = np.arange(c)[:, None]
    u = np.arange(c)[None, :]
    masks = []
    m = c
    while m >= 2:
        half = m // 2
        start = (t // m) * m
        u_start = (u // m) * m
        masks.append(((t - start) >= half) & ((u - u_start) < half) & (u_start == start))
        m //= 2
    cum = (u <= t).astype(np.float32)
    return np.concatenate([cum, cum, cum], axis=1), np.stack(masks).astype(np.float32)


def _level_signs(c, width):
    t = lax.broadcasted_iota(jnp.int32, (c, width), 0)
    signs = []
    m = c
    while m >= 8:
        signs.append(jnp.where(t % m >= m // 2, 1.0, -1.0).astype(F32))
        m //= 2
    return signs


def _level_log_decays(g, cum, signs):
    c = g.shape[0]
    out = []
    m = c
    for sgn in signs:
        half = m // 2
        mids = [jnp.broadcast_to(cum[j * m + half - 1:j * m + half, :], (m, g.shape[1])) for j in range(c // m)]
        out.append((cum - jnp.concatenate(mids, axis=0)) * sgn)
        m //= 2
    pos = lax.broadcasted_iota(jnp.int32, g.shape, 0) % 4
    g_next = pltpu.roll(g, c - 1, axis=0)
    g_prev = pltpu.roll(g, 1, axis=0)
    out.append(jnp.where(pos == 0, g_next, jnp.where(pos == 1, 0.0, jnp.where(pos == 2, g, g + g_prev))))
    return out


def _hgrn_kernel(hq_ref, hf_ref, hi_ref, hg_ref, lbl_ref, nw_ref, tab_ref, msk_ref, wn_ref, o_ref, wn_out_ref,
                 st_ref, *, chunk, heads, layer):
    c = chunk
    assert chunk >= 8 and 2 ** msk_ref.shape[0] == chunk
    ts = hq_ref.shape[1]
    wn_out_ref[...] = wn_ref[...].astype(BF16)

    @pl.when(pl.program_id(1) == 0)
    def _():
        st_ref[...] = jnp.zeros_like(st_ref)

    lbl = lbl_ref[...]
    ex = jnp.exp(lbl - jnp.max(lbl, axis=0, keepdims=True))
    lb_all = jnp.sum(ex[0:layer + 1], axis=0, keepdims=True) / jnp.sum(ex, axis=0, keepdims=True)
    nw = nw_ref[...]
    t_idx = lax.broadcasted_iota(jnp.int32, (c, c), 0)
    s_idx = lax.broadcasted_iota(jnp.int32, (c, c), 1)
    near_diag = (s_idx == t_idx) | ((t_idx % 2 == 1) & (s_idx == t_idx - 1))
    odd = lax.broadcasted_iota(jnp.int32, (c, HEAD), 0) % 2 == 1
    signs = _level_signs(c, HEAD)

    def gates(rows, h):
        cols = slice(h * HEAD, (h + 1) * HEAD)
        lb = lb_all[:, cols]
        q = _silu(hq_ref[0, rows, cols])
        f = lb + (1.0 - lb) * jax.nn.sigmoid(hf_ref[0, rows, cols])
        g = jnp.log(f) * LOG2E
        g1 = g.astype(BF16)
        r1 = g - g1.astype(F32)
        g2 = r1.astype(BF16)
        g3 = (r1 - g2.astype(F32)).astype(BF16)
        cum = _dot(tab_ref[...], jnp.concatenate([g1, g2, g3], axis=0))
        return rows, h, q, f, g, cum

    def pair_products(rows, h, q, f, g, cum):
        cols = slice(h * HEAD, (h + 1) * HEAD)
        kk = 1.0 - f
        v = hi_ref[0, rows, cols].astype(BF16)
        e_cum = jnp.exp2(cum)
        e_last = jnp.exp2(cum[c - 1:c, :] - cum)

        state_t = st_ref[h]
        o = _dot_nt((q * e_cum).astype(BF16), state_t.astype(BF16))
        k_dec = (kk * e_last).astype(BF16)
        st_ref[h] = state_t * e_cum[c - 1:c, :] + _dot_tn(v, k_dec)

        e2 = jnp.where(odd, f, 1.0)
        lhs = jnp.concatenate([q * e2, jnp.where(odd, q * (1.0 - f * f), 0.0)], axis=1).astype(BF16)
        rhs = jnp.concatenate([kk * e2, jnp.where(odd, kk, 0.0)], axis=1).astype(BF16)
        a = jnp.where(near_diag, _dot_nt(lhs, rhs), 0.0)
        nb = c // SUB
        blocks = [a[i * SUB:(i + 1) * SUB] for i in range(nb)]
        m = c
        for l, d in enumerate(_level_log_decays(g, cum, signs)):
            e = jnp.exp2(d)
            if m >= 2 * SUB:
                ups = [i for i in range(nb) if (i * SUB) % m >= m // 2]
                tile = lambda x, i: x[i * SUB:(i + 1) * SUB]
                lhs = jnp.concatenate([tile(q, i) * tile(e, i) for i in ups], axis=0)
                rhs = jnp.concatenate([tile(kk, i) if i in ups else tile(kk, i) * tile(e, i) for i in range(nb)],
                                      axis=0)
            else:
                ups = list(range(nb))
                lhs, rhs = q * e, kk * e
            sc = _dot_nt(lhs.astype(BF16), rhs.astype(BF16))
            for j, i in enumerate(ups):
                keep = msk_ref[l, i * SUB:(i + 1) * SUB, :] > 0.5
                blocks[i] = jnp.where(keep, sc[j * SUB:(j + 1) * SUB], blocks[i])
            m //= 2
        return rows, h, o, jnp.concatenate(blocks, axis=0), v

    def output(rows, h, o, a, v):
        cols = slice(h * HEAD, (h + 1) * HEAD)
        o = o + _dot(a.astype(BF16), v)
        ms = jnp.mean(o * o, axis=-1, keepdims=True)
        on = o * lax.rsqrt(ms + EPS) * nw
        o_ref[0, rows, cols] = (on * _silu(hg_ref[0, rows, cols])).astype(o_ref.dtype)

    def body(i, carry):
        after_gates, after_pairs = [], []
        for u in range(HG_UNROLL):
            rows = pl.ds(pl.multiple_of((i * HG_UNROLL + u) * c, c), c)
            for h in range(heads):
                after_gates.append(gates(rows, h))
                if len(after_gates) > HG_LOOKAHEAD:
                    after_pairs.append(pair_products(*after_gates.pop(0)))
                if len(after_pairs) > HG_LOOKAHEAD:
                    output(*after_pairs.pop(0))
        for item in after_gates:
            after_pairs.append(pair_products(*item))
        for item in after_pairs:
            output(*item)
        return carry

    lax.fori_loop(0, ts // (c * HG_UNROLL), body, 0)


def _hgrn(proj, lb_logits, norm_w, w_next, heads, layer, ts=512):
    b, s, _ = proj.shape
    nl = lb_logits.shape[0]
    w = heads * HEAD
    st = s // ts
    kn, nn = w_next.shape
    assert kn % (b * st * 16) == 0
    rows = kn // (b * st)
    tab3, msk = _hgrn_tables(HG_CHUNK)
    tab3 = jnp.asarray(tab3, BF16)
    msk = jnp.asarray(msk, F32)

    def col(group):
        return pl.BlockSpec((1, ts, w), lambda i, j, group=group: (i, j, group))

    return pl.pallas_call(
        functools.partial(_hgrn_kernel, chunk=HG_CHUNK, heads=heads, layer=layer),
        grid=(b, st),
        in_specs=[col(0), col(1), col(2), col(3),
                  pl.BlockSpec((nl, w), lambda i, j: (0, 0)),
                  pl.BlockSpec((1, HEAD), lambda i, j: (0, 0)),
                  pl.BlockSpec(tab3.shape, lambda i, j: (0, 0)),
                  pl.BlockSpec(msk.shape, lambda i, j: (0, 0, 0)),
                  pl.BlockSpec((rows, nn), lambda i, j: (i * st + j, 0))],
        out_specs=[pl.BlockSpec((1, ts, w), lambda i, j: (i, j, 0)),
                   pl.BlockSpec((rows, nn), lambda i, j: (i * st + j, 0))],
        out_shape=[jax.ShapeDtypeStruct((b, s, w), BF16), jax.ShapeDtypeStruct((kn, nn), BF16)],
        scratch_shapes=[pltpu.VMEM((heads, HEAD, HEAD), F32)],
        compiler_params=pltpu.CompilerParams(
            dimension_semantics=("arbitrary", "arbitrary"), vmem_limit_bytes=VMEM_LIMIT),
        name="hgrn2",
    )(proj, proj, proj, proj, lb_logits, norm_w, tab3, msk, w_next)


def _attn_kernel(q_ref, k_ref, v_ref, slope_ref, c_ref, wa_ref, ba_ref, o_ref, mod_ref, o_s, m_s, d_s, *, seq):
    assert DILATIONS == (1, 4, 16) and seq == 16 * N_BACK
    mod_ref[0] = _dot(_silu(c_ref[...]).astype(BF16), wa_ref[...].astype(BF16)) + ba_ref[...]
    blk = N_BACK
    npat = len(DILATIONS)
    slope = slope_ref[0]
    rows_n = 256

    qi = lax.broadcasted_iota(jnp.int32, (blk, 2 * blk), 0)
    kj = lax.broadcasted_iota(jnp.int32, (blk, 2 * blk), 1)
    steps = qi + blk - kj
    ok = (steps >= 0) & (steps <= N_BACK)
    stepsf = steps.astype(F32)

    def scores(p, dil, base, bias, with_prev):
        qb = q_ref[0, pl.ds(base, blk, stride=dil), :].astype(BF16)
        if with_prev:
            kb = k_ref[0, pl.ds(base - blk * dil, 2 * blk, stride=dil), :].astype(BF16)
            sc = _dot_nt(qb, kb) + bias
        else:
            kb = k_ref[0, pl.ds(base, blk, stride=dil), :].astype(BF16)
            sc = _dot_nt(qb, kb) + bias[:, blk:]
        return sc

    def softmax(sc, p, dil, base, with_prev):
        m = jnp.max(sc, axis=-1, keepdims=True)
        pr = jnp.exp2(sc - m).astype(BF16)
        m_s[p, pl.ds(base, blk, stride=dil), :] = jnp.broadcast_to(m, (blk, HEAD))
        return pr, p, dil, base, with_prev

    def values(pr, p, dil, base, with_prev):
        if with_prev:
            vb = v_ref[0, pl.ds(base - blk * dil, 2 * blk, stride=dil), :].astype(BF16)
        else:
            vb = v_ref[0, pl.ds(base, blk, stride=dil), :].astype(BF16)
        res = _dot(pr, jnp.concatenate([vb, jnp.ones_like(vb)], axis=1))
        o_s[p, pl.ds(base, blk, stride=dil), :] = res[:, :HEAD]
        d_s[p, pl.ds(base, blk, stride=dil), :] = res[:, HEAD:]

    def bias_for(dil):
        return jnp.where(ok, -slope[:, 0:1] * (stepsf * (float(dil) * LOG2E)), -jnp.inf)

    biases = [bias_for(d) for d in DILATIONS]
    blocks = []
    for p, dil in enumerate(DILATIONS):
        for n in range(seq // (dil * blk)):
            for r in range(dil):
                blocks.append((p, dil, r + n * dil * blk, n > 0))

    scored, probs = [], []
    for p, dil, base, with_prev in blocks:
        scored.append((scores(p, dil, base, biases[p], with_prev), p, dil, base, with_prev))
        if len(scored) > ATT_LOOKAHEAD:
            probs.append(softmax(*scored.pop(0)))
        if len(probs) > ATT_LOOKAHEAD:
            values(*probs.pop(0))
    for item in scored:
        probs.append(softmax(*item))
    for item in probs:
        values(*item)

    def merge_body(i, carry):
        rows = pl.ds(pl.multiple_of(i * rows_n, rows_n), rows_n)
        ms = [m_s[p, rows, :] for p in range(npat)]
        m = functools.reduce(jnp.maximum, ms)
        es = [jnp.exp2(mp - m) for mp in ms]
        den = functools.reduce(lambda a, b: a + b, [e * d_s[p, rows, :] for p, e in enumerate(es)])
        num = functools.reduce(lambda a, b: a + b, [e * o_s[p, rows, :] for p, e in enumerate(es)])
        o_ref[0, rows, :] = (num / den).astype(o_ref.dtype)
        return carry

    lax.fori_loop(0, seq // rows_n, merge_body, 0)


def _attention(proj, heads, col0, c_pad, w_ada, b_ada, first_chunk):
    b, s, _ = proj.shape
    rows, d = c_pad.shape
    chunks = w_ada.shape[1] // d - first_chunk
    steps = b * heads
    tc = chunks * d // steps
    per = d // tc
    assert tc * steps == chunks * d and tc % HEAD == 0 and per * tc == d
    slopes = jnp.exp2(-8.0 * jnp.arange(1, heads + 1, dtype=F32) / heads)
    slopes = jnp.broadcast_to(slopes[:, None, None], (heads, 1, HEAD))

    def col(group):
        return pl.BlockSpec((1, s, HEAD), lambda i, j, group=group: (i, 0, col0 + group * heads + j))

    npat = len(DILATIONS)
    out, mod = pl.pallas_call(
        functools.partial(_attn_kernel, seq=s),
        grid=(b, heads),
        in_specs=[col(0), col(1), col(2),
                  pl.BlockSpec((1, 1, HEAD), lambda i, j: (j, 0, 0)),
                  pl.BlockSpec((rows, d), lambda i, j: (0, 0)),
                  pl.BlockSpec((d, tc), lambda i, j: (0, first_chunk * per + i * heads + j)),
                  pl.BlockSpec((1, tc), lambda i, j: (0, first_chunk * per + i * heads + j))],
        out_specs=[pl.BlockSpec((1, s, HEAD), lambda i, j: (i, 0, j)),
                   pl.BlockSpec((1, rows, tc), lambda i, j: ((i * heads + j) // per, 0, (i * heads + j) % per))],
        out_shape=[jax.ShapeDtypeStruct((b, s, heads * HEAD), BF16), jax.ShapeDtypeStruct((chunks, rows, d), F32)],
        scratch_shapes=[pltpu.VMEM((npat, s, HEAD), F32), pltpu.VMEM((npat, s, HEAD), F32),
                        pltpu.VMEM((npat, s, HEAD), F32)],
        compiler_params=pltpu.CompilerParams(vmem_limit_bytes=VMEM_LIMIT),
        name="dilated_attn",
    )(proj, proj, proj, slopes, c_pad, w_ada, b_ada)
    return out, mod.reshape(chunks * rows, 1, d)


def _outproj_norm_kernel(a_ref, b_ref, w_ref, x_ref, gate_ref, nw_ref, scale_ref, shift_ref, x1_ref, h_ref):
    ka = a_ref.shape[1]
    mix = _dot(a_ref[...], w_ref[0:ka, :]) + _dot(b_ref[...], w_ref[ka:, :])
    x1 = x_ref[...] + gate_ref[0] * mix
    x1_ref[...] = x1
    ms = jnp.mean(x1 * x1, axis=-1, keepdims=True)
    y = x1 * lax.rsqrt(ms + EPS) * nw_ref[...]
    h_ref[...] = (y * (1.0 + scale_ref[0]) + shift_ref[0]).astype(h_ref.dtype)


def _out_projection_norm(a, b, w, x, mod, gate_row, norm_w, scale_row, shift_row, seq, tm=512):
    m, ka = a.shape
    kb = b.shape[1]
    n = w.shape[1]
    per_seq = seq // tm

    def per_batch(row):
        return pl.BlockSpec((1, 1, n), lambda i: (row + i // per_seq, 0, 0))

    return pl.pallas_call(
        _outproj_norm_kernel,
        grid=(m // tm,),
        in_specs=[pl.BlockSpec((tm, ka), lambda i: (i, 0)),
                  pl.BlockSpec((tm, kb), lambda i: (i, 0)),
                  pl.BlockSpec((ka + kb, n), lambda i: (0, 0)),
                  pl.BlockSpec((tm, n), lambda i: (i, 0)),
                  per_batch(gate_row),
                  pl.BlockSpec((1, n), lambda i: (0, 0)),
                  per_batch(scale_row), per_batch(shift_row)],
        out_specs=[pl.BlockSpec((tm, n), lambda i: (i, 0)),
                   pl.BlockSpec((tm, n), lambda i: (i, 0))],
        out_shape=[jax.ShapeDtypeStruct((m, n), F32), jax.ShapeDtypeStruct((m, n), BF16)],
        compiler_params=pltpu.CompilerParams(vmem_limit_bytes=VMEM_LIMIT),
        name="out_proj_norm",
    )(a, b, w, x, mod, norm_w, mod, mod)


def _up_kernel(h_ref, wa_ref, wg_ref, cw_ref, cb_ref, wn_ref, y_ref, wn_out_ref, wa_s, wg_s, halo_s, *, per_seq):
    i = pl.program_id(1)
    wn_out_ref[...] = wn_ref[...].astype(BF16)

    @pl.when(i == 0)
    def _():
        wa_s[...] = wa_ref[...].astype(BF16)
        wg_s[...] = wg_ref[...].astype(BF16)

    @pl.when(i % per_seq == 0)
    def _():
        halo_s[...] = jnp.zeros_like(halo_s)

    h = h_ref[...]
    a = _dot(h, wa_s[...])
    g = _dot(h, wg_s[...])
    tm = a.shape[0]
    prev = halo_s[...]
    p1 = prev[7:8, :]
    p2 = prev[6:7, :]
    row = lax.broadcasted_iota(jnp.int32, a.shape, 0)
    a1 = jnp.where(row == 0, p1, pltpu.roll(a, 1, axis=0))
    a2 = jnp.where(row == 0, p2, jnp.where(row == 1, p1, pltpu.roll(a, 2, axis=0)))
    halo_s[...] = a[tm - 8:tm, :]
    cw = cw_ref[...]
    y = cb_ref[...] + a2 * cw[0:1, :] + a1 * cw[1:2, :] + a * cw[2:3, :]
    y_ref[...] = (_silu(y) * g).astype(y_ref.dtype)


def _up_projection(h, w, conv_w, conv_b, w_next, seq, tm=1024, tn=512):
    m, k = h.shape
    dff = w.shape[1] // 2
    nb = dff // tn
    mt = m // tm
    kn, nn = w_next.shape
    assert kn % (nb * mt * 16) == 0
    rows = kn // (nb * mt)
    return pl.pallas_call(
        functools.partial(_up_kernel, per_seq=seq // tm),
        grid=(nb, mt),
        in_specs=[pl.BlockSpec((tm, k), lambda j, i: (i, 0)),
                  pl.BlockSpec((k, tn), lambda j, i: (0, j)),
                  pl.BlockSpec((k, tn), lambda j, i: (0, nb + j)),
                  pl.BlockSpec((CONV_WIDTH, tn), lambda j, i: (0, j)),
                  pl.BlockSpec((1, tn), lambda j, i: (0, j)),
                  pl.BlockSpec((rows, nn), lambda j, i: (j * mt + i, 0))],
        out_specs=[pl.BlockSpec((tm, tn), lambda j, i: (i, j)),
                   pl.BlockSpec((rows, nn), lambda j, i: (j * mt + i, 0))],
        out_shape=[jax.ShapeDtypeStruct((m, dff), BF16), jax.ShapeDtypeStruct((kn, nn), BF16)],
        scratch_shapes=[pltpu.VMEM((k, tn), BF16), pltpu.VMEM((k, tn), BF16), pltpu.VMEM((8, tn), F32)],
        compiler_params=pltpu.CompilerParams(
            dimension_semantics=("arbitrary", "arbitrary"), vmem_limit_bytes=VMEM_LIMIT),
        name="up_proj_conv_gate",
    )(h, w, w, conv_w, conv_b, w_next)


def _down_kernel(y_ref, w_ref, x_ref, gate_ref, o_ref):
    o_ref[...] = x_ref[...] + gate_ref[0] * _dot(y_ref[...], w_ref[...])


def _down_projection(y, w, x, mod, gate_row, seq, tm=512, tn=1024):
    m, k = y.shape
    n = w.shape[1]
    per_seq = seq // tm
    return pl.pallas_call(
        _down_kernel,
        grid=(n // tn, m // tm),
        in_specs=[pl.BlockSpec((tm, k), lambda j, i: (i, 0)),
                  pl.BlockSpec((k, tn), lambda j, i: (0, j)),
                  pl.BlockSpec((tm, tn), lambda j, i: (i, j)),
                  pl.BlockSpec((1, 1, tn), lambda j, i: (gate_row + i // per_seq, 0, j))],
        out_specs=pl.BlockSpec((tm, tn), lambda j, i: (i, j)),
        out_shape=jax.ShapeDtypeStruct((m, n), F32),
        compiler_params=pltpu.CompilerParams(
            dimension_semantics=("arbitrary", "arbitrary"), vmem_limit_bytes=VMEM_LIMIT),
        name="down_proj",
    )(y, w, x, mod)


def _layer(layer, x, c_pad, w_ada, b_ada, norm1_w, w_in, lb_logits, hg_norm_w, q_norm_w, k_norm_w, w_out, norm2_w,
           w_up, conv_w, conv_b, w_down):
    b, s, d = x.shape
    rows = c_pad.shape[0]
    mod1 = _modulation(c_pad, w_ada, b_ada, 2)
    shift1, scale1 = 0, rows
    gate1, shift2, scale2, gate2 = 0, rows, 2 * rows, 3 * rows

    h = _norm_modulate(x, norm1_w.reshape(1, d), mod1, scale1, shift1)
    tn = ATT_HEADS * HEAD
    q_tile = 4 * HG_HEADS * HEAD // tn
    n_tiles = w_in.shape[1] // tn
    head_w = jnp.ones((n_tiles, 1, tn), F32)
    head_w = head_w.at[q_tile, 0].set(jnp.tile(q_norm_w * (HEAD ** -0.5 * LOG2E), ATT_HEADS))
    head_w = head_w.at[q_tile + 1, 0].set(jnp.tile(k_norm_w, ATT_HEADS))
    proj = _in_projection(h.reshape(b * s, d), w_in, head_w, (q_tile, q_tile + 1), tn=tn).reshape(b, s, -1)
    a_out, w_out_bf16 = _hgrn(proj, lb_logits, hg_norm_w.reshape(1, HEAD), w_out, HG_HEADS, layer)
    b_out, mod = _attention(proj, ATT_HEADS, 4 * HG_HEADS, c_pad, w_ada, b_ada, 2)
    x1, h2 = _out_projection_norm(a_out.reshape(b * s, -1), b_out.reshape(b * s, -1), w_out_bf16,
                                  x.reshape(b * s, d), mod, gate1, norm2_w.reshape(1, d), scale2, shift2, s)
    y, w_down_bf16 = _up_projection(h2, w_up, conv_w, conv_b.reshape(1, -1), w_down, s)
    x2 = _down_projection(y, w_down_bf16, x1, mod, gate2, s)
    return x2.reshape(b, s, d)


def kernel(x, c, w_ada, b_ada, norm1_w, w_in, lb_logits, hg_norm_w, q_norm_w, k_norm_w, w_out, norm2_w, w_up, conv_w, conv_b, w_down):
    depth = w_ada.shape[0]
    b = x.shape[0]
    c_pad = jnp.pad(c, ((0, 8 - b), (0, 0)))
    for l in range(depth):
        x = _layer(l, x, c_pad, w_ada[l], b_ada[l].reshape(1, -1), norm1_w[l], w_in[l], lb_logits, hg_norm_w[l], q_norm_w[l], k_norm_w[l],
                   w_out[l], norm2_w[l], w_up[l], conv_w[l], conv_b[l], w_down[l])
    return x
```
